```python
import math
import jax, jax.numpy as jnp
from jax import lax
import numpy as np

D_MODEL = 2048
BATCH = 2
SEQ = 4096
DEPTH = 4
DEC_BATCH = 8
DEC_SEQ = 4
PAST_LEN = 16384
PAGE_SIZE = 128

HEAD_DIM = 128
MIX_WIDTH = D_MODEL
FOX_WIDTH = MIX_WIDTH // 2
FOX_HEADS = FOX_WIDTH // HEAD_DIM
CONV_CH = MIX_WIDTH - FOX_WIDTH
CONV_K = 3
DIFF_HEADS = D_MODEL // (2 * HEAD_DIM)
DIFF_V_DIM = 2 * HEAD_DIM
D_FF = 4 * D_MODEL
N_FOX_LAYERS = (DEPTH + 1) // 2
N_DIFF_LAYERS = DEPTH // 2
EVEN_IN = 3 * FOX_WIDTH + FOX_HEADS + 3 * CONV_CH
ODD_IN = 2 * (DIFF_HEADS * 2 * HEAD_DIM) + DIFF_HEADS * DIFF_V_DIM
Q_BLOCK = 128
NORM_EPS = 1e-6
NEG_INF = -1e30
FORGET_BIAS_MIN = 1.0
FORGET_BIAS_MAX = 8.0

kernel_name = 'fox_shortconv_diffattn_hybrid_step'


def rms_norm(x, gain):
    xf = x.astype(jnp.float32)
    y = xf * lax.rsqrt(jnp.mean(jnp.square(xf), axis=-1, keepdims=True) + NORM_EPS)
    return (y * gain.astype(jnp.float32)).astype(x.dtype)


def alibi_slopes(n_heads):
    return jnp.asarray([2.0 ** (-8.0 * (h + 1) / n_heads) for h in range(n_heads)], dtype=jnp.float32)


def gather_pages(pool, layer, page_table):
    g = pool[layer, page_table]
    return g.reshape(g.shape[0], g.shape[1] * g.shape[2], *g.shape[3:])


def sweep_queries(fn, q_args, q_pos):
    t = q_pos.shape[0]
    if t <= Q_BLOCK or t % Q_BLOCK != 0:
        return fn(*q_args, q_pos)
    n = t // Q_BLOCK
    blocks = tuple(jnp.moveaxis(a.reshape(a.shape[0], n, Q_BLOCK, *a.shape[2:]), 1, 0) for a in q_args)
    out = lax.map(lambda xs: fn(*xs), blocks + (q_pos.reshape(n, Q_BLOCK),))
    out = jnp.moveaxis(out, 0, 1)
    return out.reshape(out.shape[0], t, *out.shape[3:])


def fox_attention(q, k, v, c_q, c_k, q_pos):
    k_pos = jnp.arange(k.shape[1], dtype=jnp.int32)
    c_kt = jnp.swapaxes(c_k, 1, 2).astype(jnp.float32)
    scale = HEAD_DIM ** -0.5

    def block(qb, cqb, pb):
        s = jnp.einsum('bqhd,bkhd->bhqk', qb, k).astype(jnp.float32) * scale
        s = s + jnp.swapaxes(cqb, 1, 2).astype(jnp.float32)[..., :, None] - c_kt[:, :, None, :]
        mask = k_pos[None, :] <= pb[:, None]
        p = jax.nn.softmax(jnp.where(mask, s, NEG_INF), axis=-1)
        return jnp.einsum('bhqk,bkhd->bqhd', p.astype(v.dtype), v)

    return sweep_queries(block, (q, c_q), q_pos)


def diff_attention(q, k, v, lam, slopes, q_pos):
    k_pos = jnp.arange(k.shape[1], dtype=jnp.int32)
    scale = HEAD_DIM ** -0.5

    def block(qb, pb):
        s = jnp.einsum('bqhmd,bkhmd->bmhqk', qb, k).astype(jnp.float32) * scale
        dist = jnp.abs(pb[:, None] - k_pos[None, :]).astype(jnp.float32)
        s = s - slopes[:, None, None] * dist
        mask = k_pos[None, :] <= pb[:, None]
        p = jax.nn.softmax(jnp.where(mask, s, NEG_INF), axis=-1)
        w = p[:, 0] - lam * p[:, 1]
        return jnp.einsum('bhqk,bkhe->bqhe', w.astype(v.dtype), v)

    return sweep_queries(block, (q,), q_pos)


def fox_conv_mixer(h, conv_state, past, q_pos, w_in, q_gain, k_gain, f_bias, conv_w, w_out):
    b, t, _ = h.shape
    proj = jnp.einsum('btd,de->bte', h, w_in)
    o_k = FOX_WIDTH
    o_v = 2 * FOX_WIDTH
    o_f = 3 * FOX_WIDTH
    o_x = o_f + FOX_HEADS
    o_b = o_x + CONV_CH
    o_c = o_b + CONV_CH
    q = rms_norm(proj[..., :o_k].reshape(b, t, FOX_HEADS, HEAD_DIM), q_gain)
    k = rms_norm(proj[..., o_k:o_v].reshape(b, t, FOX_HEADS, HEAD_DIM), k_gain)
    v = proj[..., o_v:o_f].reshape(b, t, FOX_HEADS, HEAD_DIM)
    logf = jax.nn.log_sigmoid((proj[..., o_f:o_x] + f_bias).astype(jnp.float32))
    c_new = lax.cumsum(logf, axis=1)
    if past is None:
        k_all, v_all, c_k = k, v, c_new
    else:
        k_past, v_past, logf_past = past
        lp = logf_past.astype(jnp.float32)
        c_past = lp - lax.cumsum(lp, axis=1, reverse=True)
        k_all = jnp.concatenate([k_past.astype(k.dtype), k], axis=1)
        v_all = jnp.concatenate([v_past.astype(v.dtype), v], axis=1)
        c_k = jnp.concatenate([c_past, c_new], axis=1)
    attn = fox_attention(q, k_all, v_all, c_new, c_k, q_pos)
    cx = proj[..., o_x:o_b]
    gb = proj[..., o_b:o_c]
    gc = proj[..., o_c:]
    u = gc * cx
    u_ext = jnp.concatenate([conv_state.astype(u.dtype), u], axis=1)
    conv = sum(conv_w[tap] * u_ext[:, tap:tap + t] for tap in range(CONV_K))
    y_conv = gb * conv
    mixed = jnp.concatenate([attn.reshape(b, t, FOX_WIDTH), y_conv], axis=-1)
    out = jnp.einsum('bte,ed->btd', mixed, w_out)
    return out, (k, v, logf, u_ext[:, t:])


def diff_mixer(h, past, q_pos, layer_idx, w_in, q_gain, k_gain, lq1, lk1, lq2, lk2, subln_gain, w_out):
    b, t, _ = h.shape
    qk_w = DIFF_HEADS * 2 * HEAD_DIM
    proj = jnp.einsum('btd,de->bte', h, w_in)
    q = rms_norm(proj[..., :qk_w].reshape(b, t, DIFF_HEADS, 2, HEAD_DIM), q_gain)
    k = rms_norm(proj[..., qk_w:2 * qk_w].reshape(b, t, DIFF_HEADS, 2, HEAD_DIM), k_gain)
    v = proj[..., 2 * qk_w:].reshape(b, t, DIFF_HEADS, DIFF_V_DIM)
    lam_init = 0.8 - 0.6 * math.exp(-0.3 * layer_idx)
    lam = (jnp.exp(jnp.sum(lq1.astype(jnp.float32) * lk1.astype(jnp.float32)))
           - jnp.exp(jnp.sum(lq2.astype(jnp.float32) * lk2.astype(jnp.float32))) + lam_init)
    if past is None:
        k_all, v_all = k, v
    else:
        k_past, v_past = past
        k_all = jnp.concatenate([k_past.astype(k.dtype), k], axis=1)
        v_all = jnp.concatenate([v_past.astype(v.dtype), v], axis=1)
    o = diff_attention(q, k_all, v_all, lam, alibi_slopes(DIFF_HEADS), q_pos)
    o = rms_norm(o, subln_gain) * (1.0 - lam_init)
    out = jnp.einsum('bte,ed->btd', o.reshape(b, t, DIFF_HEADS * DIFF_V_DIM), w_out)
    return out, (k, v)


def sq_relu_mlp(h, w_up, w_down):
    a = jnp.einsum('btd,df->btf', h, w_up)
    return jnp.einsum('btf,fd->btd', jnp.square(jax.nn.relu(a)), w_down)


def setup_inputs(seed: int = 0) -> dict:
    key = jax.random.key(seed)
    ks = jax.random.split(key, 32)
    f32 = jnp.float32
    n_pages = PAST_LEN // PAGE_SIZE
    n_used = DEC_BATCH * n_pages
    n_phys = n_used + n_used // 4

    def nrm(k, shape, scale=1.0):
        return jax.random.normal(k, shape, f32) * scale

    def gain(k, shape):
        return 1.0 + 0.1 * jax.random.normal(k, shape, f32)

    x_prompt = nrm(ks[0], (BATCH, SEQ, D_MODEL))
    x_sample = nrm(ks[1], (DEC_BATCH, DEC_SEQ, D_MODEL))
    cache_fox_k = nrm(ks[2], (N_FOX_LAYERS, n_phys, PAGE_SIZE, FOX_HEADS, HEAD_DIM))
    cache_fox_v = nrm(ks[3], (N_FOX_LAYERS, n_phys, PAGE_SIZE, FOX_HEADS, HEAD_DIM))
    cache_fox_logf = jax.nn.log_sigmoid(jax.random.uniform(
        ks[4], (N_FOX_LAYERS, n_phys, PAGE_SIZE, FOX_HEADS), f32, 0.0, FORGET_BIAS_MAX + 1.0))
    state_conv = nrm(ks[5], (N_FOX_LAYERS, DEC_BATCH, CONV_K - 1, CONV_CH))
    cache_diff_k = nrm(ks[6], (N_DIFF_LAYERS, n_phys, PAGE_SIZE, DIFF_HEADS, 2, HEAD_DIM))
    cache_diff_v = nrm(ks[7], (N_DIFF_LAYERS, n_phys, PAGE_SIZE, DIFF_HEADS, DIFF_V_DIM))
    page_table = jax.random.permutation(ks[8], n_phys)[:n_used].reshape(DEC_BATCH, n_pages).astype(jnp.int32)

    attn_norm_gain = gain(ks[9], (DEPTH, D_MODEL))
    mlp_norm_gain = gain(ks[10], (DEPTH, D_MODEL))
    w_in_even = nrm(ks[11], (N_FOX_LAYERS, D_MODEL, EVEN_IN), D_MODEL ** -0.5)
    w_out_even = nrm(ks[12], (N_FOX_LAYERS, MIX_WIDTH, D_MODEL), MIX_WIDTH ** -0.5)
    fox_q_gain = gain(ks[13], (N_FOX_LAYERS, HEAD_DIM))
    fox_k_gain = gain(ks[14], (N_FOX_LAYERS, HEAD_DIM))
    fox_f_bias = jax.random.uniform(ks[15], (N_FOX_LAYERS, FOX_HEADS), f32, FORGET_BIAS_MIN, FORGET_BIAS_MAX)
    conv_w = nrm(ks[16], (N_FOX_LAYERS, CONV_K, CONV_CH), CONV_K ** -0.5)
    w_in_odd = nrm(ks[17], (N_DIFF_LAYERS, D_MODEL, ODD_IN), D_MODEL ** -0.5)
    w_out_odd = nrm(ks[18], (N_DIFF_LAYERS, DIFF_HEADS * DIFF_V_DIM, D_MODEL), (DIFF_HEADS * DIFF_V_DIM) ** -0.5)
    diff_q_gain = gain(ks[19], (N_DIFF_LAYERS, HEAD_DIM))
    diff_k_gain = gain(ks[20], (N_DIFF_LAYERS, HEAD_DIM))
    diff_lq1 = nrm(ks[21], (N_DIFF_LAYERS, HEAD_DIM), 0.1)
    diff_lk1 = nrm(ks[22], (N_DIFF_LAYERS, HEAD_DIM), 0.1)
    diff_lq2 = nrm(ks[23], (N_DIFF_LAYERS, HEAD_DIM), 0.1)
    diff_lk2 = nrm(ks[24], (N_DIFF_LAYERS, HEAD_DIM), 0.1)
    diff_subln_gain = gain(ks[25], (N_DIFF_LAYERS, DIFF_V_DIM))
    w_up = nrm(ks[26], (DEPTH, D_MODEL, D_FF), D_MODEL ** -0.5)
    w_down = nrm(ks[27], (DEPTH, D_FF, D_MODEL), D_FF ** -0.5)
    return {'x_prompt': x_prompt, 'x_sample': x_sample,
            'cache_fox_k': cache_fox_k, 'cache_fox_v': cache_fox_v, 'cache_fox_logf': cache_fox_logf,
            'state_conv': state_conv, 'cache_diff_k': cache_diff_k, 'cache_diff_v': cache_diff_v,
            'page_table': page_table,
            'attn_norm_gain': attn_norm_gain, 'mlp_norm_gain': mlp_norm_gain,
            'w_in_even': w_in_even, 'w_out_even': w_out_even,
            'fox_q_gain': fox_q_gain, 'fox_k_gain': fox_k_gain, 'fox_f_bias': fox_f_bias, 'conv_w': conv_w,
            'w_in_odd': w_in_odd, 'w_out_odd': w_out_odd,
            'diff_q_gain': diff_q_gain, 'diff_k_gain': diff_k_gain,
            'diff_lq1': diff_lq1, 'diff_lk1': diff_lk1, 'diff_lq2': diff_lq2, 'diff_lk2': diff_lk2,
            'diff_subln_gain': diff_subln_gain, 'w_up': w_up, 'w_down': w_down}


def reference(x_prompt, x_sample, cache_fox_k, cache_fox_v, cache_fox_logf, state_conv,
              cache_diff_k, cache_diff_v, page_table, attn_norm_gain, mlp_norm_gain,
              w_in_even, w_out_even, fox_q_gain, fox_k_gain, fox_f_bias, conv_w,
              w_in_odd, w_out_odd, diff_q_gain, diff_k_gain, diff_lq1, diff_lk1,
              diff_lq2, diff_lk2, diff_subln_gain, w_up, w_down):
    past_len = page_table.shape[1] * PAGE_SIZE
    pos_prompt = jnp.arange(x_prompt.shape[1], dtype=jnp.int32)
    pos_sample = past_len + jnp.arange(x_sample.shape[1], dtype=jnp.int32)

    def trunk(x, q_pos, is_sample):
        b = x.shape[0]
        fox_k, fox_v, fox_logf, conv_st, diff_k, diff_v = [], [], [], [], [], []
        for i in range(DEPTH):
            j = i // 2
            h = rms_norm(x, attn_norm_gain[i])
            if i % 2 == 0:
                if is_sample:
                    past = (gather_pages(cache_fox_k, j, page_table),
                            gather_pages(cache_fox_v, j, page_table),
                            gather_pages(cache_fox_logf, j, page_table))
                    conv_state = state_conv[j]
                else:
                    past = None
                    conv_state = jnp.zeros((b, CONV_K - 1, CONV_CH), x.dtype)
                mix, (k, v, logf, cs) = fox_conv_mixer(
                    h, conv_state, past, q_pos, w_in_even[j], fox_q_gain[j], fox_k_gain[j],
                    fox_f_bias[j], conv_w[j], w_out_even[j])
                fox_k.append(k)
                fox_v.append(v)
                fox_logf.append(logf)
                conv_st.append(cs)
            else:
                if is_sample:
                    past = (gather_pages(cache_diff_k, j, page_table),
                            gather_pages(cache_diff_v, j, page_table))
                else:
                    past = None
                mix, (k, v) = diff_mixer(
                    h, past, q_pos, i, w_in_odd[j], diff_q_gain[j], diff_k_gain[j],
                    diff_lq1[j], diff_lk1[j], diff_lq2[j], diff_lk2[j], diff_subln_gain[j], w_out_odd[j])
                diff_k.append(k)
                diff_v.append(v)
            x = x + mix
            x = x + sq_relu_mlp(rms_norm(x, mlp_norm_gain[i]), w_up[i], w_down[i])
        return (x, jnp.stack(fox_k), jnp.stack(fox_v), jnp.stack(fox_logf), jnp.stack(conv_st),
                jnp.stack(diff_k), jnp.stack(diff_v))

    y_prompt, fk_p, fv_p, fl_p, cs_p, dk_p, dv_p = trunk(x_prompt, pos_prompt, False)
    y_sample, fk_s, fv_s, fl_s, cs_s, dk_s, dv_s = trunk(x_sample, pos_sample, True)
    return (y_prompt, y_sample, fk_p, fv_p, fl_p, cs_p, dk_p, dv_p, fk_s, fv_s, fl_s, cs_s, dk_s, dv_s)
```

```python
import functools
import math

import jax
import jax.numpy as jnp
from jax import lax
from jax.experimental import pallas as pl
from jax.experimental.pallas import tpu as pltpu

F32 = jnp.float32
BF16 = jnp.bfloat16

HEAD_DIM = 128
LANES = 128
PAGE = 128
CONV_TAPS = 3
NORM_EPS = 1e-6
MASKED = -1e30
QK_SCALE = HEAD_DIM ** -0.5
VMEM_LIMIT = 56 * 1024 * 1024
HIGHEST = lax.Precision.HIGHEST


def _cparams(*sem):
    return pltpu.CompilerParams(dimension_semantics=sem, vmem_limit_bytes=VMEM_LIMIT)


def _row_tile(m, target):
    return m if m <= target else target


def _rmsnorm_kernel(x_ref, g_ref, o_ref):
    x = x_ref[...]
    ms = jnp.mean(x * x, axis=-1, keepdims=True)
    o_ref[...] = (x * lax.rsqrt(ms + NORM_EPS) * g_ref[...]).astype(o_ref.dtype)


def rmsnorm_bf16(x, gain):
    m, d = x.shape
    tm = _row_tile(m, 512)
    return pl.pallas_call(
        _rmsnorm_kernel,
        grid=(m // tm,),
        in_specs=[pl.BlockSpec((tm, d), lambda i: (i, 0)),
                  pl.BlockSpec((1, d), lambda i: (0, 0))],
        out_specs=pl.BlockSpec((tm, d), lambda i: (i, 0)),
        out_shape=jax.ShapeDtypeStruct((m, d), BF16),
        compiler_params=_cparams("parallel"),
        name="rmsnorm",
    )(x, gain.reshape(1, d))


def _mm_plain_kernel(h_ref, w_ref, o_ref):
    o_ref[...] = jnp.dot(h_ref[...], w_ref[...], preferred_element_type=F32).astype(o_ref.dtype)


def _mm_headnorm_kernel(h_ref, w_ref, g_ref, o_ref, *, scale):
    acc = jnp.dot(h_ref[...], w_ref[...], preferred_element_type=F32)
    g = g_ref[...] * scale
    for hh in range(acc.shape[1] // HEAD_DIM):
        blk = acc[:, hh * HEAD_DIM:(hh + 1) * HEAD_DIM]
        ms = jnp.mean(blk * blk, axis=-1, keepdims=True)
        o_ref[:, hh * HEAD_DIM:(hh + 1) * HEAD_DIM] = (blk * lax.rsqrt(ms + NORM_EPS) * g).astype(o_ref.dtype)


def _mm_logsigmoid_kernel(h_ref, w_ref, b_ref, o_ref):
    z = jnp.dot(h_ref[...], w_ref[...], preferred_element_type=F32) + b_ref[...]
    o_ref[...] = jnp.minimum(z, 0.0) - jnp.log1p(jnp.exp(-jnp.abs(z)))


def _mm_residual_kernel(*refs, n_lhs):
    a_refs, w_refs = refs[:n_lhs], refs[n_lhs:2 * n_lhs]
    r_ref, o_ref = refs[2 * n_lhs], refs[2 * n_lhs + 1]
    acc = r_ref[...]
    for a_ref, w_ref in zip(a_refs, w_refs):
        acc = acc + jnp.dot(a_ref[...], w_ref[...], preferred_element_type=F32)
    o_ref[...] = acc


def _mm_call(body, lhs_list, w_list, extras, extra_specs, out_dtype, name, tm_target=1024, tn_target=1024):
    m = lhs_list[0].shape[0]
    n = w_list[0].shape[1]
    tm = _row_tile(m, tm_target)
    tn = _row_tile(n, tn_target)
    in_specs = [pl.BlockSpec((tm, a.shape[1]), lambda j, i: (i, 0)) for a in lhs_list]
    in_specs += [pl.BlockSpec((w.shape[0], tn), lambda j, i: (0, j)) for w in w_list]
    in_specs += [spec(tm, tn) for spec in extra_specs]
    return pl.pallas_call(
        body,
        grid=(n // tn, m // tm),
        in_specs=in_specs,
        out_specs=pl.BlockSpec((tm, tn), lambda j, i: (i, j)),
        out_shape=jax.ShapeDtypeStruct((m, n), out_dtype),
        compiler_params=_cparams("parallel", "parallel"),
        name=name,
    )(*lhs_list, *w_list, *extras)


def _col_vec_spec(tm, tn):
    return pl.BlockSpec((1, tn), lambda j, i: (0, j))


def _tile_spec(tm, tn):
    return pl.BlockSpec((tm, tn), lambda j, i: (i, j))


def mm_plain(h, w, out_dtype=F32, name="mm_plain"):
    return _mm_call(_mm_plain_kernel, [h], [w], [], [], out_dtype, name)


def mm_headnorm(h, w, gain, scale, out_dtype, name="mm_headnorm"):
    g = gain.reshape(1, HEAD_DIM).astype(F32)
    spec = lambda tm, tn: pl.BlockSpec((1, HEAD_DIM), lambda j, i: (0, 0))
    return _mm_call(functools.partial(_mm_headnorm_kernel, scale=scale), [h], [w], [g], [spec], out_dtype, name)


def mm_logsigmoid(h, w, bias, name="mm_logf"):
    return _mm_call(_mm_logsigmoid_kernel, [h], [w], [bias.reshape(1, -1)], [_col_vec_spec], F32, name)


def mm_residual(lhs_list, w_list, res, name="mm_residual"):
    body = functools.partial(_mm_residual_kernel, n_lhs=len(lhs_list))
    return _mm_call(body, lhs_list, w_list, [res], [_tile_spec], F32, name)


def _mlp_kernel(h_ref, x_ref, wu_ref, wd_ref, o_ref):
    f = pl.program_id(1)
    a = jnp.dot(h_ref[...], wu_ref[...], preferred_element_type=F32)
    a = jnp.maximum(a, 0.0)
    part = jnp.dot((a * a).astype(BF16), wd_ref[...], preferred_element_type=F32)

    @pl.when(f == 0)
    def _():
        o_ref[...] = x_ref[...] + part

    @pl.when(f != 0)
    def _():
        o_ref[...] += part


def mlp_residual(h, x, w_up, w_down, tm_target=512, tf=1024):
    m, d = x.shape
    ff = w_up.shape[1]
    tm = _row_tile(m, tm_target)
    return pl.pallas_call(
        _mlp_kernel,
        grid=(m // tm, ff // tf),
        in_specs=[pl.BlockSpec((tm, d), lambda i, f: (i, 0)),
                  pl.BlockSpec((tm, d), lambda i, f: (i, 0)),
                  pl.BlockSpec((d, tf), lambda i, f: (0, f)),
                  pl.BlockSpec((tf, d), lambda i, f: (f, 0))],
        out_specs=pl.BlockSpec((tm, d), lambda i, f: (i, 0)),
        out_shape=jax.ShapeDtypeStruct((m, d), F32),
        compiler_params=_cparams("parallel", "arbitrary"),
        name="mlp",
    )(h, x, w_up, w_down)


def _conv_kernel(cx_ref, gb_ref, gc_ref, w_ref, st_ref, y_ref, so_ref, carry_ref, *, last):
    t = pl.program_id(1)

    @pl.when(t == 0)
    def _():
        carry_ref[...] = st_ref[...]

    u = gc_ref[...] * cx_ref[...]
    rows = lax.broadcasted_iota(jnp.int32, u.shape, 0)
    prev1 = carry_ref[1:2, :]
    prev2 = carry_ref[0:1, :]
    u1 = jnp.where(rows == 0, prev1, pltpu.roll(u, 1, axis=0))
    u2 = pltpu.roll(u, 2, axis=0)
    u2 = jnp.where(rows == 0, prev2, jnp.where(rows == 1, prev1, u2))
    conv = w_ref[0:1, :] * u2 + w_ref[1:2, :] * u1 + w_ref[2:3, :] * u
    y_ref[...] = (gb_ref[...] * conv).astype(y_ref.dtype)
    tail = u[last - 2:last, :]
    carry_ref[...] = tail
    so_ref[...] = tail


def conv_gate(proj, conv_w, state, t_valid, tt_target=512):
    b, t, c3 = proj.shape
    c = c3 // 3
    tt = _row_tile(t, tt_target)
    assert t_valid == t or t == tt
    sec = lambda k: pl.BlockSpec((None, tt, c), lambda bi, ti, k=k: (bi, ti, k))
    return pl.pallas_call(
        functools.partial(_conv_kernel, last=t_valid - (t - tt)),
        grid=(b, t // tt),
        in_specs=[sec(0), sec(1), sec(2),
                  pl.BlockSpec((CONV_TAPS, c), lambda bi, ti: (0, 0)),
                  pl.BlockSpec((None, CONV_TAPS - 1, c), lambda bi, ti: (bi, 0, 0))],
        out_specs=[pl.BlockSpec((None, tt, c), lambda bi, ti: (bi, ti, 0)),
                   pl.BlockSpec((None, CONV_TAPS - 1, c), lambda bi, ti: (bi, 0, 0))],
        out_shape=[jax.ShapeDtypeStruct((b, t, c), BF16),
                   jax.ShapeDtypeStruct((b, CONV_TAPS - 1, c), F32)],
        scratch_shapes=[pltpu.VMEM((CONV_TAPS - 1, c), F32)],
        compiler_params=_cparams("parallel", "arbitrary"),
        name="conv_gate",
    )(proj, proj, proj, conv_w, state)


def _cumsum_kernel(lf_ref, kb_ref, *, chunk):
    t = lf_ref.shape[0]
    r = lax.broadcasted_iota(jnp.int32, (chunk, chunk), 0)
    c = lax.broadcasted_iota(jnp.int32, (chunk, chunk), 1)
    upper = (r <= c).astype(F32)
    carry = jnp.zeros((LANES, 1), F32)
    for i in range(t // chunk):
        lf_t = lf_ref[i * chunk:(i + 1) * chunk, :].T
        cs = jnp.dot(lf_t, upper, precision=HIGHEST, preferred_element_type=F32) + carry
        kb_ref[:, i * chunk:(i + 1) * chunk] = -cs[:kb_ref.shape[0], :]
        carry = cs[:, chunk - 1:chunk]


def fox_key_bias(logf_padded, n_heads, chunk=512):
    b, t, w = logf_padded.shape
    return pl.pallas_call(
        functools.partial(_cumsum_kernel, chunk=chunk),
        grid=(b,),
        in_specs=[pl.BlockSpec((None, t, w), lambda bi: (bi, 0, 0))],
        out_specs=pl.BlockSpec((None, n_heads, t), lambda bi: (bi, 0, 0)),
        out_shape=jax.ShapeDtypeStruct((b, n_heads, t), F32),
        compiler_params=_cparams("parallel"),
        name="fox_cumsum",
    )(logf_padded)


def _softmax_update(z, v_bf16, m_ref, l_ref, acc_ref, g):
    m_prev = m_ref[g]
    m_new = jnp.maximum(m_prev, jnp.max(z, axis=-1, keepdims=True))
    alpha = jnp.exp(m_prev - m_new)
    p = jnp.exp(z - m_new)
    l_ref[g] = alpha * l_ref[g] + jnp.sum(p, axis=-1, keepdims=True)
    acc_ref[g] = alpha * acc_ref[g] + jnp.dot(p.astype(BF16), v_bf16, preferred_element_type=F32)
    m_ref[g] = m_new


def _causal_mask(z):
    r = lax.broadcasted_iota(jnp.int32, z.shape, 0)
    c = lax.broadcasted_iota(jnp.int32, z.shape, 1)
    return jnp.where(c <= r, z, MASKED)


def _scores_nt(q, k):
    return lax.dot_general(q, k.astype(BF16), (((1,), (1,)), ((), ())), preferred_element_type=F32)


def _flash_init(m_ref, l_ref, acc_ref):
    m_ref[...] = jnp.full(m_ref.shape, MASKED, F32)
    l_ref[...] = jnp.zeros(l_ref.shape, F32)
    acc_ref[...] = jnp.zeros(acc_ref.shape, F32)


def _fox_flash_kernel(qi_tab, ki_tab, q_ref, k_ref, v_ref, kb_ref, o_ref, m_ref, l_ref, acc_ref):
    s = pl.program_id(2)
    qi, ki = qi_tab[s], ki_tab[s]

    @pl.when(ki == 0)
    def _():
        _flash_init(m_ref, l_ref, acc_ref)

    def step(masked):
        z = _scores_nt(q_ref[...], k_ref[...]) + kb_ref[...]
        if masked:
            z = _causal_mask(z)
        _softmax_update(z, v_ref[...].astype(BF16), m_ref, l_ref, acc_ref, 0)

    @pl.when(ki < qi)
    def _():
        step(False)

    @pl.when(ki == qi)
    def _():
        step(True)
        o_ref[...] = (acc_ref[0] / l_ref[0]).astype(o_ref.dtype)


def _diff_flash_kernel(qi_tab, ki_tab, q_ref, k_ref, v_ref, slope_ref, lam_ref, gain_ref, o_ref,
                       m_ref, l_ref, acc_ref, *, out_scale):
    h = pl.program_id(1)
    s = pl.program_id(2)
    qi, ki = qi_tab[s], ki_tab[s]
    tq, tk = q_ref.shape[0], k_ref.shape[0]

    @pl.when(ki == 0)
    def _():
        _flash_init(m_ref, l_ref, acc_ref)

    def step(masked):
        kpos = lax.broadcasted_iota(jnp.int32, (1, tk), 1) + (ki * tk - qi * tq)
        kb = slope_ref[h] * kpos.astype(F32)
        v = v_ref[...].astype(BF16)
        for g in range(2):
            cols = slice(g * HEAD_DIM, (g + 1) * HEAD_DIM)
            z = _scores_nt(q_ref[:, cols], k_ref[:, cols]) + kb
            if masked:
                z = _causal_mask(z)
            _softmax_update(z, v, m_ref, l_ref, acc_ref, g)

    @pl.when(ki < qi)
    def _():
        step(False)

    @pl.when(ki == qi)
    def _():
        step(True)
        o = acc_ref[0] / l_ref[0] - lam_ref[0] * (acc_ref[1] / l_ref[1])
        ms = jnp.mean(o * o, axis=-1, keepdims=True)
        o_ref[...] = (o * lax.rsqrt(ms + NORM_EPS) * gain_ref[...] * out_scale).astype(o_ref.dtype)


def _pair_tables(nq):
    qi = [q for q in range(nq) for _ in range(q + 1)]
    ki = [k for q in range(nq) for k in range(q + 1)]
    return jnp.asarray(qi, jnp.int32), jnp.asarray(ki, jnp.int32)


def fox_flash(q, k, v, kb, n_heads, tq=512):
    b, t, _ = q.shape
    qi_tab, ki_tab = _pair_tables(t // tq)
    blk = lambda tab: pl.BlockSpec((None, tq, HEAD_DIM), lambda bi, h, s, qt, kt, tab=tab: (bi, (qt, kt)[tab][s], h))
    grid_spec = pltpu.PrefetchScalarGridSpec(
        num_scalar_prefetch=2,
        grid=(b, n_heads, qi_tab.shape[0]),
        in_specs=[blk(0), blk(1), blk(1),
                  pl.BlockSpec((None, None, 1, tq), lambda bi, h, s, qt, kt: (bi, h, 0, kt[s]))],
        out_specs=blk(0),
        scratch_shapes=[pltpu.VMEM((1, tq, 1), F32), pltpu.VMEM((1, tq, 1), F32),
                        pltpu.VMEM((1, tq, HEAD_DIM), F32)],
    )
    return pl.pallas_call(
        _fox_flash_kernel,
        grid_spec=grid_spec,
        out_shape=jax.ShapeDtypeStruct(q.shape, BF16),
        compiler_params=_cparams("parallel", "parallel", "arbitrary"),
        name="fox_flash",
    )(qi_tab, ki_tab, q, k, v, kb.reshape(b, n_heads, 1, t))


def diff_flash(q, k, v, slopes, lam, subln_gain, out_scale, n_heads, tq=512):
    b, t, _ = q.shape
    dv = 2 * HEAD_DIM
    qi_tab, ki_tab = _pair_tables(t // tq)
    blk = lambda tab: pl.BlockSpec((None, tq, dv), lambda bi, h, s, qt, kt, tab=tab: (bi, (qt, kt)[tab][s], h))
    smem = pl.BlockSpec(memory_space=pltpu.SMEM)
    grid_spec = pltpu.PrefetchScalarGridSpec(
        num_scalar_prefetch=2,
        grid=(b, n_heads, qi_tab.shape[0]),
        in_specs=[blk(0), blk(1), blk(1), smem, smem,
                  pl.BlockSpec((1, dv), lambda bi, h, s, qt, kt: (0, 0))],
        out_specs=blk(0),
        scratch_shapes=[pltpu.VMEM((2, tq, 1), F32), pltpu.VMEM((2, tq, 1), F32),
                        pltpu.VMEM((2, tq, dv), F32)],
    )
    return pl.pallas_call(
        functools.partial(_diff_flash_kernel, out_scale=out_scale),
        grid_spec=grid_spec,
        out_shape=jax.ShapeDtypeStruct(q.shape, BF16),
        compiler_params=_cparams("parallel", "parallel", "arbitrary"),
        name="diff_flash",
    )(qi_tab, ki_tab, q, k, v, slopes, lam.reshape(1), subln_gain.reshape(1, dv))


def _lambda_kernel(p_ref, o_ref, *, lam_init):
    p = p_ref[...]
    s1 = jnp.sum(p[0:1] * p[1:2], axis=-1, keepdims=True)
    s2 = jnp.sum(p[2:3] * p[3:4], axis=-1, keepdims=True)
    o_ref[...] = jnp.broadcast_to(jnp.exp(s1) - jnp.exp(s2) + lam_init, o_ref.shape)


def diff_lambda(lq1, lk1, lq2, lk2, lam_init):
    p = jnp.stack([lq1, lk1, lq2, lk2]).astype(F32)
    out = pl.pallas_call(
        functools.partial(_lambda_kernel, lam_init=lam_init),
        out_shape=jax.ShapeDtypeStruct((1, LANES), F32),
        name="diff_lambda",
    )(p)
    return out[0, 0]


def _page_suffix_kernel(lf_ref, suf_ref, tot_ref):
    lf = lf_ref[...]
    r = lax.broadcasted_iota(jnp.int32, (PAGE, PAGE), 0)
    c = lax.broadcasted_iota(jnp.int32, (PAGE, PAGE), 1)
    after = (r > c).astype(F32)
    suf_ref[...] = jnp.dot(lf, after, precision=HIGHEST, preferred_element_type=F32)
    tot_ref[...] = jnp.broadcast_to(jnp.sum(lf, axis=-1, keepdims=True), lf.shape)


def page_suffix_sums(lf_rows, tr=1024):
    rows = lf_rows.shape[0]
    tr = _row_tile(rows, tr)
    spec = pl.BlockSpec((tr, PAGE), lambda i: (i, 0))
    return pl.pallas_call(
        _page_suffix_kernel,
        grid=(rows // tr,),
        in_specs=[spec],
        out_specs=[spec, spec],
        out_shape=[jax.ShapeDtypeStruct(lf_rows.shape, F32)] * 2,
        compiler_params=_cparams("parallel"),
        name="page_suffix",
    )(lf_rows)


def _head_mismatch_bias(rows, lanes, n_heads):
    r = lax.broadcasted_iota(jnp.int32, (rows, lanes), 0) % n_heads
    c = lax.broadcasted_iota(jnp.int32, (rows, lanes), 1) % n_heads
    return jnp.where(r == c, 0.0, MASKED).astype(F32)


def _decode_step(zs, v_tiles, m_ref, l_ref, acc_ref):
    m_prev = m_ref[...]
    m_new = m_prev
    for z in zs:
        m_new = jnp.maximum(m_new, jnp.max(z, axis=-1, keepdims=True))
    alpha = jnp.exp(m_prev - m_new)
    l_new = alpha * l_ref[...]
    acc = alpha * acc_ref[...]
    for z, v in zip(zs, v_tiles):
        p = jnp.exp(z - m_new)
        l_new = l_new + jnp.sum(p, axis=-1, keepdims=True)
        acc = acc + jnp.dot(p.astype(BF16), v.astype(BF16), preferred_element_type=F32)
    m_ref[...] = m_new
    l_ref[...] = l_new
    acc_ref[...] = acc


def _new_key_mask(z, n_heads, n_tok):
    r = lax.broadcasted_iota(jnp.int32, z.shape, 0)
    c = lax.broadcasted_iota(jnp.int32, z.shape, 1)
    ok = jnp.logical_and(c // n_heads <= (r // n_heads) % n_tok, c % n_heads == r % n_heads)
    return jnp.where(ok, z, MASKED)


def _decode_fox_kernel(pt_ref, q_ref, *refs, pps, n_heads, n_tok):
    k_refs, v_refs = refs[:pps], refs[pps:2 * pps]
    suf_refs, tot_refs = refs[2 * pps:3 * pps], refs[3 * pps:4 * pps]
    knew_ref, vnew_ref, lfnew_ref, o_ref, m_ref, l_ref, acc_ref, carry_ref = refs[4 * pps:]
    c = pl.program_id(1)

    @pl.when(c == 0)
    def _():
        _flash_init(m_ref, l_ref, acc_ref)
        carry_ref[...] = jnp.zeros(carry_ref.shape, F32)

    q = q_ref[...]
    mism = _head_mismatch_bias(q.shape[0], PAGE * n_heads, n_heads)
    carry = carry_ref[...]
    zs = [None] * pps
    for j in reversed(range(pps)):
        zs[j] = _scores_nt(q, k_refs[j][...]) + (suf_refs[j][...] + carry + mism)
        carry = carry + tot_refs[j][...]
    carry_ref[...] = carry
    _decode_step(zs, [v[...] for v in v_refs], m_ref, l_ref, acc_ref)

    @pl.when(c == pl.num_programs(1) - 1)
    def _():
        lf = lfnew_ref[...]
        cum = lf
        for i in range(1, n_tok):
            cum = cum + pltpu.roll(lf, i * n_heads, axis=1)
        z = _scores_nt(q, knew_ref[...]) - cum
        _decode_step([_new_key_mask(z, n_heads, n_tok)], [vnew_ref[...]], m_ref, l_ref, acc_ref)
        o_ref[...] = acc_ref[...] / l_ref[...]


def _decode_diff_kernel(pt_ref, q_ref, *refs, pps, n_heads, n_tok, past_len, out_scale):
    k_refs, v_refs = refs[:pps], refs[pps:2 * pps]
    (knew_ref, vnew_ref, slope_ref, lam_ref, gain_ref, o_ref, m_ref, l_ref, acc_ref) = refs[2 * pps:]
    c = pl.program_id(1)
    n_chunks = pl.num_programs(1)
    half = n_tok * n_heads
    rows_per_page = PAGE * n_heads

    @pl.when(c == 0)
    def _():
        _flash_init(m_ref, l_ref, acc_ref)

    q = q_ref[...]
    slope = slope_ref[...]
    key = lax.broadcasted_iota(jnp.int32, (1, rows_per_page), 1) // n_heads
    mism = _head_mismatch_bias(2 * half, rows_per_page, n_heads)
    zs = []
    for j in range(pps):
        first = ((n_chunks - 1 - c) * pps + j) * PAGE
        bias = slope * (key + (first - past_len)).astype(F32) + mism
        z = [_scores_nt(q[g * half:(g + 1) * half], k_refs[j][pl.ds(g, rows_per_page, stride=2), :])
             for g in range(2)]
        zs.append(jnp.concatenate(z, axis=0) + bias)
    _decode_step(zs, [v[...] for v in v_refs], m_ref, l_ref, acc_ref)

    @pl.when(c == n_chunks - 1)
    def _():
        z = [_scores_nt(q[g * half:(g + 1) * half], knew_ref[g]) for g in range(2)]
        lane = lax.broadcasted_iota(jnp.int32, (1, PAGE), 1)
        z = jnp.concatenate(z, axis=0) + slope[:, :PAGE] * (lane // n_heads).astype(F32)
        _decode_step([_new_key_mask(z, n_heads, n_tok)], [vnew_ref[...]], m_ref, l_ref, acc_ref)
        o = acc_ref[...] / l_ref[...]
        o = o[:half] - lam_ref[0] * o[half:]
        ms = jnp.mean(o * o, axis=-1, keepdims=True)
        o_ref[...] = o * lax.rsqrt(ms + NORM_EPS) * gain_ref[...] * out_scale


def _paged_specs(block, layer, n_pages, pps):
    def spec(j):
        def imap(bi, c, pt):
            return (layer, pt[bi, n_pages - (c + 1) * pps + j], 0, 0)
        return pl.BlockSpec((None, None) + block, imap)
    return [spec(j) for j in range(pps)]


def decode_fox(q, k_pool, v_pool, suf_pool, tot_pool, layer, page_table, k_new, v_new, lf_new, n_heads, n_tok, pps=8):
    b, rows, _ = q.shape
    n_pages = page_table.shape[1]
    page_rows = PAGE * n_heads
    per_b = lambda shp: pl.BlockSpec((None,) + shp, lambda bi, c, pt: (bi, 0, 0))
    grid_spec = pltpu.PrefetchScalarGridSpec(
        num_scalar_prefetch=1,
        grid=(b, n_pages // pps),
        in_specs=([per_b((rows, HEAD_DIM))]
                  + _paged_specs((page_rows, HEAD_DIM), layer, n_pages, pps) * 2
                  + _paged_specs((1, page_rows), 0, n_pages, pps) * 2
                  + [per_b((PAGE, HEAD_DIM)), per_b((PAGE, HEAD_DIM)), per_b((1, PAGE))]),
        out_specs=per_b((rows, HEAD_DIM)),
        scratch_shapes=[pltpu.VMEM((rows, 1), F32), pltpu.VMEM((rows, 1), F32),
                        pltpu.VMEM((rows, HEAD_DIM), F32), pltpu.VMEM((1, page_rows), F32)],
    )
    return pl.pallas_call(
        functools.partial(_decode_fox_kernel, pps=pps, n_heads=n_heads, n_tok=n_tok),
        grid_spec=grid_spec,
        out_shape=jax.ShapeDtypeStruct((b, rows, HEAD_DIM), F32),
        compiler_params=_cparams("parallel", "arbitrary"),
        name="decode_fox",
    )(page_table, q, *([k_pool] * pps), *([v_pool] * pps), *([suf_pool] * pps), *([tot_pool] * pps),
      k_new, v_new, lf_new)


def decode_diff(q, k_pool, v_pool, layer, page_table, k_new, v_new, slope_lanes, lam, subln_gain, out_scale,
                n_heads, n_tok, pps=4):
    b, rows, _ = q.shape
    dv = 2 * HEAD_DIM
    n_pages = page_table.shape[1]
    page_rows = PAGE * n_heads
    per_b = lambda shp: pl.BlockSpec((None,) + shp, lambda bi, c, pt: (bi,) + (0,) * len(shp))
    const = lambda shp: pl.BlockSpec(shp, lambda bi, c, pt: (0, 0))
    grid_spec = pltpu.PrefetchScalarGridSpec(
        num_scalar_prefetch=1,
        grid=(b, n_pages // pps),
        in_specs=([per_b((rows, HEAD_DIM))]
                  + _paged_specs((2 * page_rows, HEAD_DIM), layer, n_pages, pps)
                  + _paged_specs((page_rows, dv), layer, n_pages, pps)
                  + [per_b((2, PAGE, HEAD_DIM)), per_b((PAGE, dv)), const((1, page_rows)),
                     pl.BlockSpec(memory_space=pltpu.SMEM), const((1, dv))]),
        out_specs=per_b((rows // 2, dv)),
        scratch_shapes=[pltpu.VMEM((rows, 1), F32), pltpu.VMEM((rows, 1), F32),
                        pltpu.VMEM((rows, dv), F32)],
    )
    return pl.pallas_call(
        functools.partial(_decode_diff_kernel, pps=pps, n_heads=n_heads, n_tok=n_tok,
                          past_len=n_pages * PAGE, out_scale=out_scale),
        grid_spec=grid_spec,
        out_shape=jax.ShapeDtypeStruct((b, rows // 2, dv), F32),
        compiler_params=_cparams("parallel", "arbitrary"),
        name="decode_diff",
    )(page_table, q, *([k_pool] * pps), *([v_pool] * pps), k_new, v_new,
      slope_lanes, lam.reshape(1), subln_gain.reshape(1, dv))


def _pad_rows(x, rows):
    pad = [(0, 0)] * x.ndim
    pad[-2] = (0, rows - x.shape[-2])
    return jnp.pad(x, pad)


def kernel(x_prompt, x_sample, cache_fox_k, cache_fox_v, cache_fox_logf, state_conv, cache_diff_k, cache_diff_v,
           page_table, attn_norm_gain, mlp_norm_gain, w_in_even, w_out_even, fox_q_gain, fox_k_gain, fox_f_bias,
           conv_w, w_in_odd, w_out_odd, diff_q_gain, diff_k_gain, diff_lq1, diff_lk1, diff_lq2, diff_lk2,
           diff_subln_gain, w_up, w_down):
    bp, tp, d = x_prompt.shape
    bs, ts, _ = x_sample.shape
    depth = attn_norm_gain.shape[0]
    n_fox = cache_fox_k.shape[3]
    fox_w = n_fox * HEAD_DIM
    conv_c = state_conv.shape[-1]
    n_diff = cache_diff_k.shape[3]
    qk_w = n_diff * 2 * HEAD_DIM
    dv = cache_diff_v.shape[-1]
    n_phys = cache_fox_k.shape[1]

    fox_k_pool = cache_fox_k.reshape(-1, n_phys, PAGE * n_fox, HEAD_DIM)
    fox_v_pool = cache_fox_v.reshape(-1, n_phys, PAGE * n_fox, HEAD_DIM)
    diff_k_pool = cache_diff_k.reshape(-1, n_phys, PAGE * n_diff * 2, HEAD_DIM)
    diff_v_pool = cache_diff_v.reshape(-1, n_phys, PAGE * n_diff, dv)
    slopes = jnp.asarray([2.0 ** (-8.0 * (h + 1) / n_diff) for h in range(n_diff)], F32)
    slope_lanes = jnp.tile(slopes, PAGE).reshape(1, PAGE * n_diff)

    xp = x_prompt.reshape(bp * tp, d)
    xs = x_sample.reshape(bs * ts, d)
    ts_pad = 8
    leaves = {k: [] for k in ("fk_p", "fv_p", "fl_p", "cs_p", "dk_p", "dv_p",
                              "fk_s", "fv_s", "fl_s", "cs_s", "dk_s", "dv_s")}

    for i in range(depth):
        j = i // 2
        hp = rmsnorm_bf16(xp, attn_norm_gain[i])
        hs = rmsnorm_bf16(xs, attn_norm_gain[i])
        if i % 2 == 0:
            w = w_in_even[j]
            o_k, o_v, o_f = fox_w, 2 * fox_w, 3 * fox_w
            o_x = o_f + n_fox
            wq, wk, wv = (w[:, :o_k].astype(BF16), w[:, o_k:o_v].astype(BF16), w[:, o_v:o_f].astype(BF16))
            wf = jnp.pad(w[:, o_f:o_x], ((0, 0), (0, LANES - n_fox))).astype(BF16)
            wc = w[:, o_x:].astype(BF16)
            bias = jnp.pad(fox_f_bias[j], (0, LANES - n_fox))
            wo = w_out_even[j].astype(BF16)
            lf_rows = jnp.swapaxes(cache_fox_logf[j], 1, 2).reshape(n_phys * n_fox, PAGE)
            suf, tot = page_suffix_sums(lf_rows)
            to_lanes = lambda a: jnp.swapaxes(a.reshape(n_phys, n_fox, PAGE), 1, 2).reshape(1, n_phys, 1, PAGE * n_fox)
            suf_pool, tot_pool = to_lanes(suf), to_lanes(tot)
            results = []
            for h_act, x_res, sample in ((hp, xp, False), (hs, xs, True)):
                qn = mm_headnorm(h_act, wq, fox_q_gain[j], QK_SCALE, BF16, name="fox_q")
                kn = mm_headnorm(h_act, wk, fox_k_gain[j], 1.0, F32, name="fox_k")
                vv = mm_plain(h_act, wv, name="fox_v")
                lf = mm_logsigmoid(h_act, wf, bias)
                cproj = mm_plain(h_act, wc, name="conv_proj")
                if not sample:
                    kb = fox_key_bias(lf.reshape(bp, tp, LANES), n_fox)
                    attn = fox_flash(qn.reshape(bp, tp, fox_w), kn.reshape(bp, tp, fox_w),
                                     vv.reshape(bp, tp, fox_w), kb, n_fox).reshape(bp * tp, fox_w)
                    yc, cs = conv_gate(cproj.reshape(bp, tp, 3 * conv_c), conv_w[j],
                                       jnp.zeros((bp, CONV_TAPS - 1, conv_c), F32), tp)
                    yc = yc.reshape(bp * tp, conv_c)
                    leaves["fk_p"].append(kn.reshape(bp, tp, n_fox, HEAD_DIM))
                    leaves["fv_p"].append(vv.reshape(bp, tp, n_fox, HEAD_DIM))
                    leaves["fl_p"].append(lf.reshape(bp, tp, LANES)[..., :n_fox])
                    leaves["cs_p"].append(cs)
                else:
                    lf_s = lf.reshape(bs, ts, LANES)[..., :n_fox]
                    lf_new = _pad_rows(lf_s.reshape(bs, ts * n_fox, 1), PAGE).reshape(bs, 1, PAGE)
                    attn = decode_fox(qn.reshape(bs, ts * n_fox, HEAD_DIM), fox_k_pool, fox_v_pool, suf_pool, tot_pool,
                                      j, page_table,
                                      _pad_rows(kn.reshape(bs, ts * n_fox, HEAD_DIM), PAGE),
                                      _pad_rows(vv.reshape(bs, ts * n_fox, HEAD_DIM), PAGE),
                                      lf_new, n_fox, ts)
                    attn = attn.reshape(bs * ts, fox_w).astype(BF16)
                    cpad = _pad_rows(cproj.reshape(bs, ts, 3 * conv_c), ts_pad)
                    yc, cs = conv_gate(cpad, conv_w[j], state_conv[j], ts)
                    yc = yc[:, :ts].reshape(bs * ts, conv_c)
                    leaves["fk_s"].append(kn.reshape(bs, ts, n_fox, HEAD_DIM))
                    leaves["fv_s"].append(vv.reshape(bs, ts, n_fox, HEAD_DIM))
                    leaves["fl_s"].append(lf_s)
                    leaves["cs_s"].append(cs)
                results.append(mm_residual([attn, yc], [wo[:fox_w], wo[fox_w:]], x_res, name="even_out"))
            xp, xs = results
        else:
            w = w_in_odd[j]
            wq, wk, wv = (w[:, :qk_w].astype(BF16), w[:, qk_w:2 * qk_w].astype(BF16), w[:, 2 * qk_w:].astype(BF16))
            wo = w_out_odd[j].astype(BF16)
            lam_init = 0.8 - 0.6 * math.exp(-0.3 * i)
            lam = diff_lambda(diff_lq1[j], diff_lk1[j], diff_lq2[j], diff_lk2[j], lam_init)
            results = []
            for h_act, x_res, sample in ((hp, xp, False), (hs, xs, True)):
                qn = mm_headnorm(h_act, wq, diff_q_gain[j], QK_SCALE, BF16, name="diff_q")
                kn = mm_headnorm(h_act, wk, diff_k_gain[j], 1.0, F32, name="diff_k")
                vv = mm_plain(h_act, wv, name="diff_v")
                if not sample:
                    o = diff_flash(qn.reshape(bp, tp, qk_w), kn.reshape(bp, tp, qk_w), vv.reshape(bp, tp, n_diff * dv),
                                   slopes, lam, diff_subln_gain[j], 1.0 - lam_init, n_diff)
                    o = o.reshape(bp * tp, n_diff * dv)
                    leaves["dk_p"].append(kn.reshape(bp, tp, n_diff, 2, HEAD_DIM))
                    leaves["dv_p"].append(vv.reshape(bp, tp, n_diff, dv))
                else:
                    split = lambda a: jnp.moveaxis(a.reshape(bs, ts, n_diff, 2, HEAD_DIM), 3, 1)
                    q_rows = split(qn).reshape(bs, 2 * ts * n_diff, HEAD_DIM)
                    k_new = _pad_rows(split(kn).reshape(bs, 2, ts * n_diff, HEAD_DIM), PAGE)
                    v_new = _pad_rows(vv.reshape(bs, ts * n_diff, dv), PAGE)
                    o = decode_diff(q_rows, diff_k_pool, diff_v_pool, j, page_table, k_new, v_new,
                                    slope_lanes, lam, diff_subln_gain[j], 1.0 - lam_init, n_diff, ts)
                    o = o.reshape(bs * ts, n_diff * dv).astype(BF16)
                    leaves["dk_s"].append(kn.reshape(bs, ts, n_diff, 2, HEAD_DIM))
                    leaves["dv_s"].append(vv.reshape(bs, ts, n_diff, dv))
                results.append(mm_residual([o], [wo], x_res, name="odd_out"))
            xp, xs = results
        wu, wd = w_up[i].astype(BF16), w_down[i].astype(BF16)
        xp = mlp_residual(rmsnorm_bf16(xp, mlp_norm_gain[i]), xp, wu, wd)
        xs = mlp_residual(rmsnorm_bf16(xs, mlp_norm_gain[i]), xs, wu, wd)

    st = {k: jnp.stack(v) for k, v in leaves.items()}
    return (xp.reshape(bp, tp, d), xs.reshape(bs, ts, d),
            st["fk_p"], st["fv_p"], st["fl_p"], st["cs_p"], st["dk_p"], st["dv_p"],
            st["fk_s"], st["fv_s"], st["fl_s"], st["cs_s"], st["dk_s"], st["dv_s"])
```

```python
import functools
import math

import jax
import jax.numpy as jnp
from jax import lax
from jax.experimental import pallas as pl
from jax.experimental.pallas import tpu as pltpu

F32 = jnp.float32
BF16 = jnp.bfloat16

HEAD_DIM = 128
LANES = 128
SUBLANES = 8
PAGE = 128
CONV_TAPS = 3
NORM_EPS = 1e-6
MASKED = -1e30
QK_SCALE = HEAD_DIM ** -0.5
LOG2E = math.log2(math.e)
VMEM_LIMIT = 56 * 1024 * 1024
HIGHEST = lax.Precision.HIGHEST


def _cparams(*sem):
    return pltpu.CompilerParams(dimension_semantics=sem, vmem_limit_bytes=VMEM_LIMIT)


def _row_tile(m, target):
    return m if m <= target else target


def _rms_normed(x, gain):
    ms = jnp.mean(x * x, axis=-1, keepdims=True)
    return x * lax.rsqrt(ms + NORM_EPS) * gain


def _rmsnorm_kernel(x_ref, g_ref, o_ref):
    o_ref[...] = _rms_normed(x_ref[...], g_ref[...]).astype(o_ref.dtype)


def rmsnorm_bf16(x, gain):
    m, d = x.shape
    tm = _row_tile(m, 512)
    return pl.pallas_call(
        _rmsnorm_kernel,
        grid=(m // tm,),
        in_specs=[pl.BlockSpec((tm, d), lambda i: (i, 0)),
                  pl.BlockSpec((1, d), lambda i: (0, 0))],
        out_specs=pl.BlockSpec((tm, d), lambda i: (i, 0)),
        out_shape=jax.ShapeDtypeStruct((m, d), BF16),
        compiler_params=_cparams("parallel"),
        name="rmsnorm",
    )(x, gain.reshape(1, d))


def _cached_bf16(w_ref, wb_ref):
    @pl.when(pl.program_id(1) == 0)
    def _():
        wb_ref[...] = w_ref[...].astype(BF16)


def _proj_headnorm_kernel(*refs, scale, leaf, alias):
    h_ref, w_ref, g_ref = refs[:3]
    outs = refs[3 + alias:]
    o_ref, wb_ref = outs[0], outs[-1]
    _cached_bf16(w_ref, wb_ref)
    acc = jnp.dot(h_ref[...], wb_ref[...], preferred_element_type=F32)
    g = g_ref[...]
    for hh in range(acc.shape[1] // HEAD_DIM):
        cols = slice(hh * HEAD_DIM, (hh + 1) * HEAD_DIM)
        blk = acc[:, cols]
        y = blk * lax.rsqrt(jnp.mean(blk * blk, axis=-1, keepdims=True) + NORM_EPS) * g
        if leaf:
            outs[1][:, hh, :] = y
        o_ref[:, cols] = (y * scale).astype(o_ref.dtype)


def _proj_plain_kernel(*refs, leaf_rows, alias):
    h_ref, w_ref = refs[:2]
    outs = refs[2 + alias:]
    o_ref, wb_ref = outs[0], outs[-1]
    _cached_bf16(w_ref, wb_ref)
    acc = jnp.dot(h_ref[...], wb_ref[...], preferred_element_type=F32)
    o_ref[...] = acc.astype(o_ref.dtype)
    if leaf_rows is not None:
        for cc, row in enumerate(leaf_rows):
            outs[1][:, row, :] = acc[:, cc * HEAD_DIM:(cc + 1) * HEAD_DIM]


def _proj_logsigmoid_kernel(h_ref, w_ref, b_ref, o_ref, wb_ref):
    _cached_bf16(w_ref, wb_ref)
    z = jnp.dot(h_ref[...], wb_ref[...], preferred_element_type=F32) + b_ref[...]
    o_ref[...] = jnp.minimum(z, 0.0) - jnp.log1p(jnp.exp(-jnp.abs(z)))


def _proj_residual_kernel(*refs, n_lhs):
    a_refs, w_refs = refs[:n_lhs], refs[n_lhs:2 * n_lhs]
    r_ref, o_ref = refs[2 * n_lhs], refs[2 * n_lhs + 1]
    wb_refs = refs[2 * n_lhs + 2:]
    acc = r_ref[...]
    for a_ref, w_ref, wb_ref in zip(a_refs, w_refs, wb_refs):
        _cached_bf16(w_ref, wb_ref)
        acc = acc + jnp.dot(a_ref[...], wb_ref[...], preferred_element_type=F32)
    o_ref[...] = acc


def _weight_spec(k_rows, tn, layer, row_blk, col_blk0, mode=None):
    kw = {} if mode is None else {"pipeline_mode": mode}
    return pl.BlockSpec((None, k_rows, tn), lambda j, i: (layer, row_blk, col_blk0 + j), **kw)


def _lhs_spec(tm, k):
    return pl.BlockSpec((tm, k), lambda j, i: (i, 0))


def _leaf_args(prev, n_layers, m, groups, layer, tm, chunks):
    spec = pl.BlockSpec((None, tm, chunks, LANES), lambda j, i: (layer, i, j, 0))
    shape = jax.ShapeDtypeStruct((n_layers, m, groups, LANES), F32)
    return spec, shape


def proj_heads(h, w3, layer, col0, n_cols, gain=None, scale=1.0, leaf=None, leaf_prev=None, leaf_rows=None,
               tm=512, tn=1024, single_buffer=False, name="proj"):
    m, k = h.shape
    tm, tn = _row_tile(m, tm), _row_tile(n_cols, tn)
    chunks = tn // HEAD_DIM
    alias = leaf_prev is not None
    mode = pl.Buffered(1) if single_buffer else None
    in_specs = [_lhs_spec(tm, k), _weight_spec(k, tn, layer, 0, col0 // tn, mode)]
    args = [h, w3]
    if gain is not None:
        in_specs.append(pl.BlockSpec((1, HEAD_DIM), lambda j, i: (0, 0)))
        args.append(gain.reshape(1, HEAD_DIM).astype(F32))
    out_specs = [pl.BlockSpec((tm, tn), lambda j, i: (i, j))]
    out_shape = [jax.ShapeDtypeStruct((m, n_cols), BF16)]
    aliases = {}
    if leaf is not None:
        n_layers, groups = leaf
        rows_per_tile = chunks if leaf_rows is None else groups
        spec, shape = _leaf_args(leaf_prev, n_layers, m, groups, layer, tm, rows_per_tile)
        out_specs.append(spec)
        out_shape.append(shape)
        if alias:
            in_specs.append(pl.BlockSpec(memory_space=pl.ANY))
            args.append(leaf_prev)
            aliases = {len(args) - 1: 1}
    if gain is not None:
        body = functools.partial(_proj_headnorm_kernel, scale=scale, leaf=leaf is not None, alias=int(alias))
    else:
        rows = None if leaf is None else (leaf_rows if leaf_rows is not None else tuple(range(chunks)))
        body = functools.partial(_proj_plain_kernel, leaf_rows=rows, alias=int(alias))
    outs = pl.pallas_call(
        body,
        grid=(n_cols // tn, m // tm),
        in_specs=in_specs,
        out_specs=out_specs,
        out_shape=out_shape,
        scratch_shapes=[pltpu.VMEM((k, tn), BF16)],
        input_output_aliases=aliases,
        compiler_params=_cparams("parallel", "arbitrary"),
        name=name,
    )(*args)
    return outs if leaf is not None else outs[0]


def proj_logsigmoid(h, w, bias, name="proj_logf"):
    m, k = h.shape
    n = w.shape[1]
    tm = _row_tile(m, 1024)
    return pl.pallas_call(
        _proj_logsigmoid_kernel,
        grid=(1, m // tm),
        in_specs=[_lhs_spec(tm, k), _weight_spec(k, n, 0, 0, 0), pl.BlockSpec((1, n), lambda j, i: (0, 0))],
        out_specs=pl.BlockSpec((tm, n), lambda j, i: (i, 0)),
        out_shape=jax.ShapeDtypeStruct((m, n), F32),
        scratch_shapes=[pltpu.VMEM((k, n), BF16)],
        compiler_params=_cparams("parallel", "arbitrary"),
        name=name,
    )(h, w[None], bias.reshape(1, n))


def proj_f32(h, w, tm=1024, tn=1024, name="proj_f32"):
    m, k = h.shape
    n = w.shape[1]
    tm, tn = _row_tile(m, tm), _row_tile(n, tn)
    return pl.pallas_call(
        functools.partial(_proj_plain_kernel, leaf_rows=None, alias=0),
        grid=(n // tn, m // tm),
        in_specs=[_lhs_spec(tm, k), _weight_spec(k, tn, 0, 0, 0)],
        out_specs=pl.BlockSpec((tm, tn), lambda j, i: (i, j)),
        out_shape=jax.ShapeDtypeStruct((m, n), F32),
        scratch_shapes=[pltpu.VMEM((k, tn), BF16)],
        compiler_params=_cparams("parallel", "arbitrary"),
        name=name,
    )(h, w[None])


def proj_residual(lhs_list, w3, layer, res, tm=512, tn=1024, name="proj_out"):
    m, n = res.shape
    tm, tn = _row_tile(m, tm), _row_tile(n, tn)
    k = lhs_list[0].shape[1]
    in_specs = [_lhs_spec(tm, k) for _ in lhs_list]
    in_specs += [_weight_spec(k, tn, layer, r, 0) for r in range(len(lhs_list))]
    in_specs.append(pl.BlockSpec((tm, tn), lambda j, i: (i, j)))
    return pl.pallas_call(
        functools.partial(_proj_residual_kernel, n_lhs=len(lhs_list)),
        grid=(n // tn, m // tm),
        in_specs=in_specs,
        out_specs=pl.BlockSpec((tm, tn), lambda j, i: (i, j)),
        out_shape=jax.ShapeDtypeStruct((m, n), F32),
        scratch_shapes=[pltpu.VMEM((k, tn), BF16) for _ in lhs_list],
        compiler_params=_cparams("parallel", "arbitrary"),
        name=name,
    )(*lhs_list, *([w3] * len(lhs_list)), res)


def _mlp_kernel(x_ref, g_ref, wu_ref, wd_ref, o_ref, h_ref):
    f = pl.program_id(1)

    @pl.when(f == 0)
    def _():
        h_ref[...] = _rms_normed(x_ref[...], g_ref[...]).astype(BF16)

    a = jnp.dot(h_ref[...], wu_ref[...], preferred_element_type=F32)
    a = jnp.maximum(a, 0.0)
    part = jnp.dot((a * a).astype(BF16), wd_ref[...], preferred_element_type=F32)

    @pl.when(f == 0)
    def _():
        o_ref[...] = x_ref[...] + part

    @pl.when(f != 0)
    def _():
        o_ref[...] += part


def mlp_residual(x, gain, w_up, w_down, tm_target=512, tf=1024):
    m, d = x.shape
    ff = w_up.shape[1]
    tm = _row_tile(m, tm_target)
    return pl.pallas_call(
        _mlp_kernel,
        grid=(m // tm, ff // tf),
        in_specs=[pl.BlockSpec((tm, d), lambda i, f: (i, 0)),
                  pl.BlockSpec((1, d), lambda i, f: (0, 0)),
                  pl.BlockSpec((d, tf), lambda i, f: (0, f)),
                  pl.BlockSpec((tf, d), lambda i, f: (f, 0))],
        out_specs=pl.BlockSpec((tm, d), lambda i, f: (i, 0)),
        out_shape=jax.ShapeDtypeStruct((m, d), F32),
        scratch_shapes=[pltpu.VMEM((tm, d), BF16)],
        compiler_params=_cparams("parallel", "arbitrary"),
        name="mlp",
    )(x, gain.reshape(1, d), w_up, w_down)


def _conv_kernel(cx_ref, gb_ref, gc_ref, w_ref, st_ref, y_ref, so_ref, carry_ref, *, last):
    t = pl.program_id(1)

    @pl.when(t == 0)
    def _():
        carry_ref[...] = st_ref[...]

    u = gc_ref[...] * cx_ref[...]
    rows = lax.broadcasted_iota(jnp.int32, u.shape, 0)
    prev1 = carry_ref[1:2, :]
    prev2 = carry_ref[0:1, :]
    u1 = jnp.where(rows == 0, prev1, pltpu.roll(u, 1, axis=0))
    u2 = pltpu.roll(u, 2, axis=0)
    u2 = jnp.where(rows == 0, prev2, jnp.where(rows == 1, prev1, u2))
    conv = w_ref[0:1, :] * u2 + w_ref[1:2, :] * u1 + w_ref[2:3, :] * u
    y_ref[...] = (gb_ref[...] * conv).astype(y_ref.dtype)
    tail = u[last - 2:last, :]
    carry_ref[...] = tail
    so_ref[...] = tail


def conv_gate(proj, conv_w, state, t_valid, tt_target=512):
    b, t, c3 = proj.shape
    c = c3 // 3
    tt = _row_tile(t, tt_target)
    assert t_valid == t or t == tt
    sec = lambda k: pl.BlockSpec((None, tt, c), lambda bi, ti, k=k: (bi, ti, k))
    return pl.pallas_call(
        functools.partial(_conv_kernel, last=t_valid - (t - tt)),
        grid=(b, t // tt),
        in_specs=[sec(0), sec(1), sec(2),
                  pl.BlockSpec((CONV_TAPS, c), lambda bi, ti: (0, 0)),
                  pl.BlockSpec((None, CONV_TAPS - 1, c), lambda bi, ti: (bi, 0, 0))],
        out_specs=[pl.BlockSpec((None, tt, c), lambda bi, ti: (bi, ti, 0)),
                   pl.BlockSpec((None, CONV_TAPS - 1, c), lambda bi, ti: (bi, 0, 0))],
        out_shape=[jax.ShapeDtypeStruct((b, t, c), BF16),
                   jax.ShapeDtypeStruct((b, CONV_TAPS - 1, c), F32)],
        scratch_shapes=[pltpu.VMEM((CONV_TAPS - 1, c), F32)],
        compiler_params=_cparams("parallel", "arbitrary"),
        name="conv_gate",
    )(proj, proj, proj, conv_w, state)


def _cumsum_kernel(lf_ref, kb_ref, *, chunk):
    t = lf_ref.shape[0]
    r = lax.broadcasted_iota(jnp.int32, (chunk, chunk), 0)
    c = lax.broadcasted_iota(jnp.int32, (chunk, chunk), 1)
    upper = (r <= c).astype(F32)
    carry = jnp.zeros((LANES, 1), F32)
    for i in range(t // chunk):
        lf_t = lf_ref[i * chunk:(i + 1) * chunk, :].T
        cs = jnp.dot(lf_t, upper, precision=HIGHEST, preferred_element_type=F32) + carry
        kb_ref[:, i * chunk:(i + 1) * chunk] = cs[:kb_ref.shape[0], :] * (-LOG2E)
        carry = cs[:, chunk - 1:chunk]


def fox_key_bias(logf_padded, n_heads, chunk=512):
    b, t, w = logf_padded.shape
    chunk = _row_tile(t, chunk)
    return pl.pallas_call(
        functools.partial(_cumsum_kernel, chunk=chunk),
        grid=(b,),
        in_specs=[pl.BlockSpec((None, t, w), lambda bi: (bi, 0, 0))],
        out_specs=pl.BlockSpec((None, n_heads, t), lambda bi: (bi, 0, 0)),
        out_shape=jax.ShapeDtypeStruct((b, n_heads, t), F32),
        compiler_params=_cparams("parallel"),
        name="fox_cumsum",
    )(logf_padded)


def _lane_tile(x, width):
    return x if width == LANES else jnp.concatenate([x] * (width // LANES), axis=1)


def _softmax_update(z, v, m_ref, l_ref, acc_ref, g):
    m_prev = m_ref[g]
    m_new = jnp.maximum(m_prev, jnp.max(z, axis=-1, keepdims=True))
    alpha = jnp.exp2(m_prev - m_new)
    p = jnp.exp2(z - _lane_tile(m_new, z.shape[1]))
    l_ref[g] = alpha * l_ref[g] + jnp.sum(p, axis=-1, keepdims=True)
    pv = jnp.dot(p.astype(BF16), v, preferred_element_type=F32)
    acc_ref[g] = _lane_tile(alpha, pv.shape[1]) * acc_ref[g] + pv
    m_ref[g] = m_new


def _causal_mask(z):
    r = lax.broadcasted_iota(jnp.int32, z.shape, 0)
    c = lax.broadcasted_iota(jnp.int32, z.shape, 1)
    return jnp.where(c <= r, z, MASKED)


def _scores_nt(q, k):
    return lax.dot_general(q, k.astype(BF16), (((1,), (1,)), ((), ())), preferred_element_type=F32)


def _flash_init(m_ref, l_ref, acc_ref):
    m_ref[...] = jnp.full(m_ref.shape, MASKED, F32)
    l_ref[...] = jnp.zeros(l_ref.shape, F32)
    acc_ref[...] = jnp.zeros(acc_ref.shape, F32)


def _fox_flash_kernel(qi_tab, ki_tab, q_ref, k_ref, v_ref, kb_ref, o_ref, m_ref, l_ref, acc_ref, *, hp):
    s = pl.program_id(2)
    qi, ki = qi_tab[s], ki_tab[s]

    @pl.when(ki == 0)
    def _():
        _flash_init(m_ref, l_ref, acc_ref)

    def step(masked):
        for g in range(hp):
            cols = slice(g * HEAD_DIM, (g + 1) * HEAD_DIM)
            z = _scores_nt(q_ref[:, cols], k_ref[:, cols]) + kb_ref[g]
            if masked:
                z = _causal_mask(z)
            _softmax_update(z, v_ref[:, cols], m_ref, l_ref, acc_ref, g)

    @pl.when(ki < qi)
    def _():
        step(False)

    @pl.when(ki == qi)
    def _():
        step(True)
        for g in range(hp):
            o_ref[:, g * HEAD_DIM:(g + 1) * HEAD_DIM] = (acc_ref[g] / l_ref[g]).astype(o_ref.dtype)


def _diff_flash_kernel(qi_tab, ki_tab, q_ref, k_ref, v_ref, slope_ref, lam_ref, gain_ref, o_ref,
                       m_ref, l_ref, acc_ref, *, out_scale):
    h = pl.program_id(1)
    s = pl.program_id(2)
    qi, ki = qi_tab[s], ki_tab[s]
    tq, tk = q_ref.shape[0], k_ref.shape[0]

    @pl.when(ki == 0)
    def _():
        _flash_init(m_ref, l_ref, acc_ref)

    def step(masked):
        kpos = lax.broadcasted_iota(jnp.int32, (1, tk), 1) + (ki * tk - qi * tq)
        kb = (slope_ref[h] * LOG2E) * kpos.astype(F32)
        v = v_ref[...]
        for g in range(2):
            cols = slice(g * HEAD_DIM, (g + 1) * HEAD_DIM)
            z = _scores_nt(q_ref[:, cols], k_ref[:, cols]) + kb
            if masked:
                z = _causal_mask(z)
            _softmax_update(z, v, m_ref, l_ref, acc_ref, g)

    @pl.when(ki < qi)
    def _():
        step(False)

    @pl.when(ki == qi)
    def _():
        step(True)
        dv = acc_ref.shape[2]
        o = acc_ref[0] / _lane_tile(l_ref[0], dv) - lam_ref[0] * (acc_ref[1] / _lane_tile(l_ref[1], dv))
        o_ref[...] = (_rms_normed(o, gain_ref[...]) * out_scale).astype(o_ref.dtype)


def _pair_tables(nq):
    qi = [q for q in range(nq) for _ in range(q + 1)]
    ki = [k for q in range(nq) for k in range(q + 1)]
    return jnp.asarray(qi, jnp.int32), jnp.asarray(ki, jnp.int32)


def fox_flash(q, k, v, kb, n_heads, tq=1024, hp=2):
    b, t, _ = q.shape
    tq = _row_tile(t, tq)
    w = hp * HEAD_DIM
    qi_tab, ki_tab = _pair_tables(t // tq)
    blk = lambda tab: pl.BlockSpec((None, tq, w), lambda bi, h, s, qt, kt, tab=tab: (bi, (qt, kt)[tab][s], h))
    grid_spec = pltpu.PrefetchScalarGridSpec(
        num_scalar_prefetch=2,
        grid=(b, n_heads // hp, qi_tab.shape[0]),
        in_specs=[blk(0), blk(1), blk(1),
                  pl.BlockSpec((None, hp, 1, tq), lambda bi, h, s, qt, kt: (bi, h, 0, kt[s]))],
        out_specs=blk(0),
        scratch_shapes=[pltpu.VMEM((hp, tq, LANES), F32), pltpu.VMEM((hp, tq, LANES), F32),
                        pltpu.VMEM((hp, tq, HEAD_DIM), F32)],
    )
    return pl.pallas_call(
        functools.partial(_fox_flash_kernel, hp=hp),
        grid_spec=grid_spec,
        out_shape=jax.ShapeDtypeStruct(q.shape, BF16),
        compiler_params=_cparams("parallel", "parallel", "arbitrary"),
        name="fox_flash",
    )(qi_tab, ki_tab, q, k, v, kb.reshape(b, n_heads, 1, t))


def diff_flash(q, k, v, slopes, lam, subln_gain, out_scale, n_heads, tq=1024):
    b, t, _ = q.shape
    tq = _row_tile(t, tq)
    dv = 2 * HEAD_DIM
    qi_tab, ki_tab = _pair_tables(t // tq)
    blk = lambda tab: pl.BlockSpec((None, tq, dv), lambda bi, h, s, qt, kt, tab=tab: (bi, (qt, kt)[tab][s], h))
    smem = pl.BlockSpec(memory_space=pltpu.SMEM)
    grid_spec = pltpu.PrefetchScalarGridSpec(
        num_scalar_prefetch=2,
        grid=(b, n_heads, qi_tab.shape[0]),
        in_specs=[blk(0), blk(1), blk(1), smem, smem,
                  pl.BlockSpec((1, dv), lambda bi, h, s, qt, kt: (0, 0))],
        out_specs=blk(0),
        scratch_shapes=[pltpu.VMEM((2, tq, LANES), F32), pltpu.VMEM((2, tq, LANES), F32),
                        pltpu.VMEM((2, tq, dv), F32)],
    )
    return pl.pallas_call(
        functools.partial(_diff_flash_kernel, out_scale=out_scale),
        grid_spec=grid_spec,
        out_shape=jax.ShapeDtypeStruct(q.shape, BF16),
        compiler_params=_cparams("parallel", "parallel", "arbitrary"),
        name="diff_flash",
    )(qi_tab, ki_tab, q, k, v, slopes, lam.reshape(1), subln_gain.reshape(1, dv))


def _lambda_kernel(p_ref, o_ref, *, lam_init):
    p = p_ref[...]
    s1 = jnp.sum(p[0:1] * p[1:2], axis=-1, keepdims=True)
    s2 = jnp.sum(p[2:3] * p[3:4], axis=-1, keepdims=True)
    o_ref[...] = jnp.broadcast_to(jnp.exp(s1) - jnp.exp(s2) + lam_init, o_ref.shape)


def diff_lambda(lq1, lk1, lq2, lk2, lam_init):
    p = jnp.stack([lq1, lk1, lq2, lk2]).astype(F32)
    out = pl.pallas_call(
        functools.partial(_lambda_kernel, lam_init=lam_init),
        out_shape=jax.ShapeDtypeStruct((1, LANES), F32),
        name="diff_lambda",
    )(p)
    return out[0, 0]


def _page_suffix_kernel(lf_ref, suf_ref, tot_ref):
    lf = lf_ref[...]
    r = lax.broadcasted_iota(jnp.int32, (PAGE, PAGE), 0)
    c = lax.broadcasted_iota(jnp.int32, (PAGE, PAGE), 1)
    after = (r > c).astype(F32)
    suf_ref[...] = jnp.dot(lf, after, precision=HIGHEST, preferred_element_type=F32)
    tot_ref[...] = jnp.broadcast_to(jnp.sum(lf, axis=-1, keepdims=True), lf.shape)


def page_suffix_sums(lf_rows, tr=1024):
    rows = lf_rows.shape[0]
    tr = _row_tile(rows, tr)
    spec = pl.BlockSpec((tr, PAGE), lambda i: (i, 0))
    return pl.pallas_call(
        _page_suffix_kernel,
        grid=(rows // tr,),
        in_specs=[spec],
        out_specs=[spec, spec],
        out_shape=[jax.ShapeDtypeStruct(lf_rows.shape, F32)] * 2,
        compiler_params=_cparams("parallel"),
        name="page_suffix",
    )(lf_rows)


def _head_mismatch_bias(rows, lanes, n_heads):
    r = lax.broadcasted_iota(jnp.int32, (rows, lanes), 0) % n_heads
    c = lax.broadcasted_iota(jnp.int32, (rows, lanes), 1) % n_heads
    return jnp.where(r == c, 0.0, MASKED).astype(F32)


def _decode_step(zs, v_tiles, m_ref, l_ref, acc_ref):
    m_prev = m_ref[...]
    m_new = m_prev
    for z in zs:
        m_new = jnp.maximum(m_new, jnp.max(z, axis=-1, keepdims=True))
    alpha = jnp.exp(m_prev - m_new)
    l_new = alpha * l_ref[...]
    acc = alpha * acc_ref[...]
    for z, v in zip(zs, v_tiles):
        p = jnp.exp(z - m_new)
        l_new = l_new + jnp.sum(p, axis=-1, keepdims=True)
        acc = acc + jnp.dot(p.astype(BF16), v.astype(BF16), preferred_element_type=F32)
    m_ref[...] = m_new
    l_ref[...] = l_new
    acc_ref[...] = acc


def _new_key_mask(z, n_heads, n_tok):
    r = lax.broadcasted_iota(jnp.int32, z.shape, 0)
    c = lax.broadcasted_iota(jnp.int32, z.shape, 1)
    ok = jnp.logical_and(c // n_heads <= (r // n_heads) % n_tok, c % n_heads == r % n_heads)
    return jnp.where(ok, z, MASKED)


def _decode_fox_kernel(pt_ref, q_ref, *refs, pps, n_heads, n_tok):
    k_refs, v_refs = refs[:pps], refs[pps:2 * pps]
    suf_refs, tot_refs = refs[2 * pps:3 * pps], refs[3 * pps:4 * pps]
    knew_ref, vnew_ref, lfnew_ref, o_ref, m_ref, l_ref, acc_ref, carry_ref = refs[4 * pps:]
    c = pl.program_id(1)

    @pl.when(c == 0)
    def _():
        _flash_init(m_ref, l_ref, acc_ref)
        carry_ref[...] = jnp.zeros(carry_ref.shape, F32)

    q = q_ref[...]
    mism = _head_mismatch_bias(q.shape[0], PAGE * n_heads, n_heads)
    carry = carry_ref[...]
    zs = [None] * pps
    for j in reversed(range(pps)):
        zs[j] = _scores_nt(q, k_refs[j][...]) + (suf_refs[j][...] + carry + mism)
        carry = carry + tot_refs[j][...]
    carry_ref[...] = carry
    _decode_step(zs, [v[...] for v in v_refs], m_ref, l_ref, acc_ref)

    @pl.when(c == pl.num_programs(1) - 1)
    def _():
        lf = lfnew_ref[...]
        cum = lf
        for i in range(1, n_tok):
            cum = cum + pltpu.roll(lf, i * n_heads, axis=1)
        z = _scores_nt(q, knew_ref[...]) - cum
        _decode_step([_new_key_mask(z, n_heads, n_tok)], [vnew_ref[...]], m_ref, l_ref, acc_ref)
        o_ref[...] = acc_ref[...] / l_ref[...]


def _decode_diff_kernel(pt_ref, q_ref, *refs, pps, n_heads, n_tok, past_len, out_scale):
    k_refs, v_refs = refs[:pps], refs[pps:2 * pps]
    (knew_ref, vnew_ref, slope_ref, lam_ref, gain_ref, o_ref, m_ref, l_ref, acc_ref) = refs[2 * pps:]
    c = pl.program_id(1)
    n_chunks = pl.num_programs(1)
    half = n_tok * n_heads
    rows_per_page = PAGE * n_heads

    @pl.when(c == 0)
    def _():
        _flash_init(m_ref, l_ref, acc_ref)

    q = q_ref[...]
    slope = slope_ref[...]
    key = lax.broadcasted_iota(jnp.int32, (1, rows_per_page), 1) // n_heads
    mism = _head_mismatch_bias(2 * half, rows_per_page, n_heads)
    zs = []
    for j in range(pps):
        first = ((n_chunks - 1 - c) * pps + j) * PAGE
        bias = slope * (key + (first - past_len)).astype(F32) + mism
        z = [_scores_nt(q[g * half:(g + 1) * half], k_refs[j][pl.ds(g, rows_per_page, stride=2), :])
             for g in range(2)]
        zs.append(jnp.concatenate(z, axis=0) + bias)
    _decode_step(zs, [v[...] for v in v_refs], m_ref, l_ref, acc_ref)

    @pl.when(c == n_chunks - 1)
    def _():
        z = [_scores_nt(q[g * half:(g + 1) * half], knew_ref[g]) for g in range(2)]
        lane = lax.broadcasted_iota(jnp.int32, (1, PAGE), 1)
        z = jnp.concatenate(z, axis=0) + slope[:, :PAGE] * (lane // n_heads).astype(F32)
        _decode_step([_new_key_mask(z, n_heads, n_tok)], [vnew_ref[...]], m_ref, l_ref, acc_ref)
        o = acc_ref[...] / l_ref[...]
        o = o[:half] - lam_ref[0] * o[half:]
        o_ref[...] = _rms_normed(o, gain_ref[...]) * out_scale


def _paged_specs(block, layer, n_pages, pps):
    def spec(j):
        def imap(bi, c, pt):
            return (layer, pt[bi, n_pages - (c + 1) * pps + j], 0, 0)
        return pl.BlockSpec((None, None) + block, imap)
    return [spec(j) for j in range(pps)]


def decode_fox(q, k_pool, v_pool, suf_pool, tot_pool, layer, page_table, k_new, v_new, lf_new, n_heads, n_tok, pps=8):
    b, rows, _ = q.shape
    n_pages = page_table.shape[1]
    page_rows = PAGE * n_heads
    per_b = lambda shp: pl.BlockSpec((None,) + shp, lambda bi, c, pt: (bi, 0, 0))
    grid_spec = pltpu.PrefetchScalarGridSpec(
        num_scalar_prefetch=1,
        grid=(b, n_pages // pps),
        in_specs=([per_b((rows, HEAD_DIM))]
                  + _paged_specs((page_rows, HEAD_DIM), layer, n_pages, pps) * 2
                  + _paged_specs((1, page_rows), 0, n_pages, pps) * 2
                  + [per_b((PAGE, HEAD_DIM)), per_b((PAGE, HEAD_DIM)), per_b((1, PAGE))]),
        out_specs=per_b((rows, HEAD_DIM)),
        scratch_shapes=[pltpu.VMEM((rows, 1), F32), pltpu.VMEM((rows, 1), F32),
                        pltpu.VMEM((rows, HEAD_DIM), F32), pltpu.VMEM((1, page_rows), F32)],
    )
    return pl.pallas_call(
        functools.partial(_decode_fox_kernel, pps=pps, n_heads=n_heads, n_tok=n_tok),
        grid_spec=grid_spec,
        out_shape=jax.ShapeDtypeStruct((b, rows, HEAD_DIM), F32),
        compiler_params=_cparams("parallel", "arbitrary"),
        name="decode_fox",
    )(page_table, q, *([k_pool] * pps), *([v_pool] * pps), *([suf_pool] * pps), *([tot_pool] * pps),
      k_new, v_new, lf_new)


def decode_diff(q, k_pool, v_pool, layer, page_table, k_new, v_new, slope_lanes, lam, subln_gain, out_scale,
                n_heads, n_tok, pps=4):
    b, rows, _ = q.shape
    dv = 2 * HEAD_DIM
    n_pages = page_table.shape[1]
    page_rows = PAGE * n_heads
    per_b = lambda shp: pl.BlockSpec((None,) + shp, lambda bi, c, pt: (bi,) + (0,) * len(shp))
    const = lambda shp: pl.BlockSpec(shp, lambda bi, c, pt: (0, 0))
    grid_spec = pltpu.PrefetchScalarGridSpec(
        num_scalar_prefetch=1,
        grid=(b, n_pages // pps),
        in_specs=([per_b((rows, HEAD_DIM))]
                  + _paged_specs((2 * page_rows, HEAD_DIM), layer, n_pages, pps)
                  + _paged_specs((page_rows, dv), layer, n_pages, pps)
                  + [per_b((2, PAGE, HEAD_DIM)), per_b((PAGE, dv)), const((1, page_rows)),
                     pl.BlockSpec(memory_space=pltpu.SMEM), const((1, dv))]),
        out_specs=per_b((rows // 2, dv)),
        scratch_shapes=[pltpu.VMEM((rows, 1), F32), pltpu.VMEM((rows, 1), F32),
                        pltpu.VMEM((rows, dv), F32)],
    )
    return pl.pallas_call(
        functools.partial(_decode_diff_kernel, pps=pps, n_heads=n_heads, n_tok=n_tok,
                          past_len=n_pages * PAGE, out_scale=out_scale),
        grid_spec=grid_spec,
        out_shape=jax.ShapeDtypeStruct((b, rows // 2, dv), F32),
        compiler_params=_cparams("parallel", "arbitrary"),
        name="decode_diff",
    )(page_table, q, *([k_pool] * pps), *([v_pool] * pps), k_new, v_new,
      slope_lanes, lam.reshape(1), subln_gain.reshape(1, dv))


def _pad_rows(x, rows):
    pad = [(0, 0)] * x.ndim
    pad[-2] = (0, rows - x.shape[-2])
    return jnp.pad(x, pad)


def kernel(x_prompt, x_sample, cache_fox_k, cache_fox_v, cache_fox_logf, state_conv, cache_diff_k, cache_diff_v,
           page_table, attn_norm_gain, mlp_norm_gain, w_in_even, w_out_even, fox_q_gain, fox_k_gain, fox_f_bias,
           conv_w, w_in_odd, w_out_odd, diff_q_gain, diff_k_gain, diff_lq1, diff_lk1, diff_lq2, diff_lk2,
           diff_subln_gain, w_up, w_down):
    bp, tp, d = x_prompt.shape
    bs, ts, _ = x_sample.shape
    depth = attn_norm_gain.shape[0]
    n_fox_layers, n_diff_layers = w_in_even.shape[0], w_in_odd.shape[0]
    n_fox = cache_fox_k.shape[3]
    fox_w = n_fox * HEAD_DIM
    conv_c = state_conv.shape[-1]
    n_diff = cache_diff_k.shape[3]
    qk_w = n_diff * 2 * HEAD_DIM
    dv = cache_diff_v.shape[-1]
    v_w = n_diff * dv
    n_phys = cache_fox_k.shape[1]
    diff_v_rows = tuple((c % 2) * n_diff + c // 2 for c in range(v_w // HEAD_DIM))

    fox_k_pool = cache_fox_k.reshape(-1, n_phys, PAGE * n_fox, HEAD_DIM)
    fox_v_pool = cache_fox_v.reshape(-1, n_phys, PAGE * n_fox, HEAD_DIM)
    diff_k_pool = cache_diff_k.reshape(-1, n_phys, PAGE * n_diff * 2, HEAD_DIM)
    diff_v_pool = cache_diff_v.reshape(-1, n_phys, PAGE * n_diff, dv)
    slopes = jnp.asarray([2.0 ** (-8.0 * (h + 1) / n_diff) for h in range(n_diff)], F32)
    slope_lanes = jnp.tile(slopes, PAGE).reshape(1, PAGE * n_diff)

    xp = x_prompt.reshape(bp * tp, d)
    xs = x_sample.reshape(bs * ts, d)
    mp, ms = bp * tp, bs * ts
    ts_pad = SUBLANES
    leaf = {k: None for k in ("fk_p", "fv_p", "dk_p", "dv_p", "fk_s", "fv_s", "dk_s", "dv_s")}
    small = {k: [] for k in ("fl_p", "cs_p", "fl_s", "cs_s")}

    for i in range(depth):
        j = i // 2
        hp = rmsnorm_bf16(xp, attn_norm_gain[i])
        hs = rmsnorm_bf16(xs, attn_norm_gain[i])
        if i % 2 == 0:
            o_f = 3 * fox_w
            o_x = o_f + n_fox
            w = w_in_even[j]
            wf = jnp.pad(w[:, o_f:o_x], ((0, 0), (0, LANES - n_fox))).astype(BF16)
            wc = w[:, o_x:].astype(BF16)
            bias = jnp.pad(fox_f_bias[j], (0, LANES - n_fox))
            lf_rows = jnp.swapaxes(cache_fox_logf[j], 1, 2).reshape(n_phys * n_fox, PAGE)
            suf, tot = page_suffix_sums(lf_rows)
            to_lanes = lambda a: jnp.swapaxes(a.reshape(n_phys, n_fox, PAGE), 1, 2).reshape(1, n_phys, 1, PAGE * n_fox)
            suf_pool, tot_pool = to_lanes(suf), to_lanes(tot)
            results = []
            for h_act, x_res, sample in ((hp, xp, False), (hs, xs, True)):
                tag = "_s" if sample else "_p"
                q_scale = QK_SCALE if sample else QK_SCALE * LOG2E
                qn = proj_heads(h_act, w_in_even, j, 0, fox_w, gain=fox_q_gain[j], scale=q_scale, name="fox_q")
                kn, leaf["fk" + tag] = proj_heads(h_act, w_in_even, j, fox_w, fox_w, gain=fox_k_gain[j],
                                                  leaf=(n_fox_layers, n_fox), leaf_prev=leaf["fk" + tag], name="fox_k")
                vv, leaf["fv" + tag] = proj_heads(h_act, w_in_even, j, 2 * fox_w, fox_w,
                                                  leaf=(n_fox_layers, n_fox), leaf_prev=leaf["fv" + tag], name="fox_v")
                lf = proj_logsigmoid(h_act, wf, bias)
                cproj = proj_f32(h_act, wc, name="conv_proj")
                if not sample:
                    kb = fox_key_bias(lf.reshape(bp, tp, LANES), n_fox)
                    attn = fox_flash(qn.reshape(bp, tp, fox_w), kn.reshape(bp, tp, fox_w),
                                     vv.reshape(bp, tp, fox_w), kb, n_fox).reshape(mp, fox_w)
                    yc, cs = conv_gate(cproj.reshape(bp, tp, 3 * conv_c), conv_w[j],
                                       jnp.zeros((bp, CONV_TAPS - 1, conv_c), F32), tp)
                    yc = yc.reshape(mp, conv_c)
                    small["fl_p"].append(lf.reshape(bp, tp, LANES)[..., :n_fox])
                    small["cs_p"].append(cs)
                else:
                    lf_s = lf.reshape(bs, ts, LANES)[..., :n_fox]
                    lf_new = _pad_rows(lf_s.reshape(bs, ts * n_fox, 1), PAGE).reshape(bs, 1, PAGE)
                    new_rows = lambda lv: _pad_rows(lv[j].reshape(bs, ts * n_fox, HEAD_DIM), PAGE)
                    attn = decode_fox(qn.reshape(bs, ts * n_fox, HEAD_DIM), fox_k_pool, fox_v_pool, suf_pool, tot_pool,
                                      j, page_table, new_rows(leaf["fk_s"]), new_rows(leaf["fv_s"]), lf_new, n_fox, ts)
                    attn = attn.reshape(ms, fox_w).astype(BF16)
                    cpad = _pad_rows(cproj.reshape(bs, ts, 3 * conv_c), ts_pad)
                    yc, cs = conv_gate(cpad, conv_w[j], state_conv[j], ts)
                    yc = yc[:, :ts].reshape(ms, conv_c)
                    small["fl_s"].append(lf_s)
                    small["cs_s"].append(cs)
                results.append(proj_residual([attn, yc], w_out_even, j, x_res, name="even_out"))
            xp, xs = results
        else:
            lam_init = 0.8 - 0.6 * math.exp(-0.3 * i)
            lam = diff_lambda(diff_lq1[j], diff_lk1[j], diff_lq2[j], diff_lk2[j], lam_init)
            results = []
            for h_act, x_res, sample in ((hp, xp, False), (hs, xs, True)):
                tag = "_s" if sample else "_p"
                q_scale = QK_SCALE if sample else QK_SCALE * LOG2E
                qn = proj_heads(h_act, w_in_odd, j, 0, qk_w, gain=diff_q_gain[j], scale=q_scale, name="diff_q")
                kn, leaf["dk" + tag] = proj_heads(h_act, w_in_odd, j, qk_w, qk_w, gain=diff_k_gain[j],
                                                  leaf=(n_diff_layers, qk_w // HEAD_DIM), leaf_prev=leaf["dk" + tag],
                                                  name="diff_k")
                vv, leaf["dv" + tag] = proj_heads(h_act, w_in_odd, j, 2 * qk_w, v_w,
                                                  leaf=(n_diff_layers, v_w // HEAD_DIM), leaf_prev=leaf["dv" + tag],
                                                  leaf_rows=diff_v_rows, tm=512, tn=v_w, single_buffer=True,
                                                  name="diff_v")
                if not sample:
                    o = diff_flash(qn.reshape(bp, tp, qk_w), kn.reshape(bp, tp, qk_w), vv.reshape(bp, tp, v_w),
                                   slopes, lam, diff_subln_gain[j], 1.0 - lam_init, n_diff)
                    o = o.reshape(mp, v_w)
                else:
                    split = lambda a: jnp.moveaxis(a.reshape(bs, ts, n_diff, 2, HEAD_DIM), 3, 1)
                    q_rows = split(qn).reshape(bs, 2 * ts * n_diff, HEAD_DIM)
                    k_new = _pad_rows(split(leaf["dk_s"][j]).reshape(bs, 2, ts * n_diff, HEAD_DIM), PAGE)
                    v_new = jnp.swapaxes(leaf["dv_s"][j].reshape(bs, ts, 2, n_diff, HEAD_DIM), 2, 3)
                    v_new = _pad_rows(v_new.reshape(bs, ts * n_diff, dv), PAGE)
                    o = decode_diff(q_rows, diff_k_pool, diff_v_pool, j, page_table, k_new, v_new,
                                    slope_lanes, lam, diff_subln_gain[j], 1.0 - lam_init, n_diff, ts)
                    o = o.reshape(ms, v_w).astype(BF16)
                results.append(proj_residual([o], w_out_odd, j, x_res, name="odd_out"))
            xp, xs = results
        wu, wd = w_up[i].astype(BF16), w_down[i].astype(BF16)
        xp = mlp_residual(xp, mlp_norm_gain[i], wu, wd)
        xs = mlp_residual(xs, mlp_norm_gain[i], wu, wd)

    st = {k: jnp.stack(v) for k, v in small.items()}
    fox_leaf = lambda a, b, t: a.reshape(n_fox_layers, b, t, n_fox, HEAD_DIM)
    dk_leaf = lambda a, b, t: a.reshape(n_diff_layers, b, t, n_diff, 2, HEAD_DIM)
    dv_leaf = lambda a, b, t: jnp.swapaxes(a.reshape(n_diff_layers, b, t, 2, n_diff, HEAD_DIM), 3, 4).reshape(
        n_diff_layers, b, t, n_diff, dv)
    return (xp.reshape(bp, tp, d), xs.reshape(bs, ts, d),
            fox_leaf(leaf["fk_p"], bp, tp), fox_leaf(leaf["fv_p"], bp, tp), st["fl_p"], st["cs_p"],
            dk_leaf(leaf["dk_p"], bp, tp), dv_leaf(leaf["dv_p"], bp, tp),
            fox_leaf(leaf["fk_s"], bs, ts), fox_leaf(leaf["fv_s"], bs, ts), st["fl_s"], st["cs_s"],
            dk_leaf(leaf["dk_s"], bs, ts), dv_leaf(leaf["dv_s"], bs, ts))
```

```python
import functools
import math

import jax
import jax.numpy as jnp
from jax import lax
from jax.experimental import pallas as pl
from jax.experimental.pallas import tpu as pltpu

F32 = jnp.float32
BF16 = jnp.bfloat16

HEAD_DIM = 128
LANES = 128
SUBLANES = 8
PAGE = 128
CONV_TAPS = 3
NORM_EPS = 1e-6
MASKED = -1e30
QK_SCALE = HEAD_DIM ** -0.5
LOG2E = math.log2(math.e)
VMEM_LIMIT = 56 * 1024 * 1024
HIGHEST = lax.Precision.HIGHEST


def _cparams(*sem):
    return pltpu.CompilerParams(dimension_semantics=sem, vmem_limit_bytes=VMEM_LIMIT)


def _row_tile(m, target):
    return m if m <= target else target


def _rms_normed(x, gain):
    ms = jnp.mean(x * x, axis=-1, keepdims=True)
    return x * lax.rsqrt(ms + NORM_EPS) * gain


def _rmsnorm_kernel(x_ref, g_ref, o_ref):
    o_ref[...] = _rms_normed(x_ref[...], g_ref[...]).astype(o_ref.dtype)


def rmsnorm_bf16(x, gain):
    m, d = x.shape
    tm = _row_tile(m, 512)
    return pl.pallas_call(
        _rmsnorm_kernel,
        grid=(m // tm,),
        in_specs=[pl.BlockSpec((tm, d), lambda i: (i, 0)),
                  pl.BlockSpec((1, d), lambda i: (0, 0))],
        out_specs=pl.BlockSpec((tm, d), lambda i: (i, 0)),
        out_shape=jax.ShapeDtypeStruct((m, d), BF16),
        compiler_params=_cparams("parallel"),
        name="rmsnorm",
    )(x, gain.reshape(1, d))


def _cached_bf16(w_ref, wb_ref):
    @pl.when(pl.program_id(1) == 0)
    def _():
        wb_ref[...] = w_ref[...].astype(BF16)


def _proj_headnorm_kernel(*refs, scale, leaf, alias):
    h_ref, w_ref, g_ref = refs[:3]
    outs = refs[3 + alias:]
    o_ref, wb_ref = outs[0], outs[-1]
    _cached_bf16(w_ref, wb_ref)
    acc = jnp.dot(h_ref[...], wb_ref[...], preferred_element_type=F32)
    g = g_ref[...]
    for hh in range(acc.shape[1] // HEAD_DIM):
        cols = slice(hh * HEAD_DIM, (hh + 1) * HEAD_DIM)
        blk = acc[:, cols]
        y = blk * lax.rsqrt(jnp.mean(blk * blk, axis=-1, keepdims=True) + NORM_EPS) * g
        if leaf:
            outs[1][:, hh, :] = y
        o_ref[:, cols] = (y * scale).astype(o_ref.dtype)


def _proj_plain_kernel(*refs, leaf_rows, alias):
    h_ref, w_ref = refs[:2]
    outs = refs[2 + alias:]
    o_ref, wb_ref = outs[0], outs[-1]
    _cached_bf16(w_ref, wb_ref)
    acc = jnp.dot(h_ref[...], wb_ref[...], preferred_element_type=F32)
    o_ref[...] = acc.astype(o_ref.dtype)
    if leaf_rows is not None:
        for cc, row in enumerate(leaf_rows):
            outs[1][:, row, :] = acc[:, cc * HEAD_DIM:(cc + 1) * HEAD_DIM]


def _proj_logsigmoid_kernel(h_ref, w_ref, b_ref, o_ref, wb_ref):
    _cached_bf16(w_ref, wb_ref)
    z = jnp.dot(h_ref[...], wb_ref[...], preferred_element_type=F32) + b_ref[...]
    o_ref[...] = jnp.minimum(z, 0.0) - jnp.log1p(jnp.exp(-jnp.abs(z)))


def _proj_residual_kernel(*refs, n_lhs):
    a_refs, w_refs = refs[:n_lhs], refs[n_lhs:2 * n_lhs]
    r_ref, o_ref = refs[2 * n_lhs], refs[2 * n_lhs + 1]
    wb_refs = refs[2 * n_lhs + 2:]
    acc = r_ref[...]
    for a_ref, w_ref, wb_ref in zip(a_refs, w_refs, wb_refs):
        _cached_bf16(w_ref, wb_ref)
        acc = acc + jnp.dot(a_ref[...], wb_ref[...], preferred_element_type=F32)
    o_ref[...] = acc


def _weight_spec(k_rows, tn, layer, row_blk, col_blk0):
    return pl.BlockSpec((None, k_rows, tn), lambda j, i: (layer, row_blk, col_blk0 + j),
                        pipeline_mode=pl.Buffered(1))


def _lhs_spec(tm, k):
    return pl.BlockSpec((tm, k), lambda j, i: (i, 0))


def _leaf_args(prev, n_layers, m, groups, layer, tm, chunks):
    spec = pl.BlockSpec((None, tm, chunks, LANES), lambda j, i: (layer, i, j, 0))
    shape = jax.ShapeDtypeStruct((n_layers, m, groups, LANES), F32)
    return spec, shape


def proj_heads(h, w3, layer, col0, n_cols, gain=None, scale=1.0, leaf=None, leaf_prev=None, leaf_rows=None,
               tm=1024, tn=1024, name="proj"):
    m, k = h.shape
    tm, tn = _row_tile(m, tm), _row_tile(n_cols, tn)
    chunks = tn // HEAD_DIM
    alias = leaf_prev is not None
    in_specs = [_lhs_spec(tm, k), _weight_spec(k, tn, layer, 0, col0 // tn)]
    args = [h, w3]
    if gain is not None:
        in_specs.append(pl.BlockSpec((1, HEAD_DIM), lambda j, i: (0, 0)))
        args.append(gain.reshape(1, HEAD_DIM).astype(F32))
    out_specs = [pl.BlockSpec((tm, tn), lambda j, i: (i, j))]
    out_shape = [jax.ShapeDtypeStruct((m, n_cols), BF16)]
    aliases = {}
    if leaf is not None:
        n_layers, groups = leaf
        rows_per_tile = chunks if leaf_rows is None else groups
        spec, shape = _leaf_args(leaf_prev, n_layers, m, groups, layer, tm, rows_per_tile)
        out_specs.append(spec)
        out_shape.append(shape)
        if alias:
            in_specs.append(pl.BlockSpec(memory_space=pl.ANY))
            args.append(leaf_prev)
            aliases = {len(args) - 1: 1}
    if gain is not None:
        body = functools.partial(_proj_headnorm_kernel, scale=scale, leaf=leaf is not None, alias=int(alias))
    else:
        rows = None if leaf is None else (leaf_rows if leaf_rows is not None else tuple(range(chunks)))
        body = functools.partial(_proj_plain_kernel, leaf_rows=rows, alias=int(alias))
    outs = pl.pallas_call(
        body,
        grid=(n_cols // tn, m // tm),
        in_specs=in_specs,
        out_specs=out_specs,
        out_shape=out_shape,
        scratch_shapes=[pltpu.VMEM((k, tn), BF16)],
        input_output_aliases=aliases,
        compiler_params=_cparams("parallel", "arbitrary"),
        name=name,
    )(*args)
    return outs if leaf is not None else outs[0]


def proj_logsigmoid(h, w3, layer, bias, name="proj_logf"):
    m, k = h.shape
    n = w3.shape[2]
    tm = _row_tile(m, 1024)
    return pl.pallas_call(
        _proj_logsigmoid_kernel,
        grid=(1, m // tm),
        in_specs=[_lhs_spec(tm, k), _weight_spec(k, n, layer, 0, 0), pl.BlockSpec((1, n), lambda j, i: (0, 0))],
        out_specs=pl.BlockSpec((tm, n), lambda j, i: (i, 0)),
        out_shape=jax.ShapeDtypeStruct((m, n), F32),
        scratch_shapes=[pltpu.VMEM((k, n), BF16)],
        compiler_params=_cparams("parallel", "arbitrary"),
        name=name,
    )(h, w3, bias.reshape(1, n))


def proj_f32(h, w3, layer, tm=1024, tn=1024, name="proj_f32"):
    m, k = h.shape
    n = w3.shape[2]
    tm, tn = _row_tile(m, tm), _row_tile(n, tn)
    return pl.pallas_call(
        functools.partial(_proj_plain_kernel, leaf_rows=None, alias=0),
        grid=(n // tn, m // tm),
        in_specs=[_lhs_spec(tm, k), _weight_spec(k, tn, layer, 0, 0)],
        out_specs=pl.BlockSpec((tm, tn), lambda j, i: (i, j)),
        out_shape=jax.ShapeDtypeStruct((m, n), F32),
        scratch_shapes=[pltpu.VMEM((k, tn), BF16)],
        compiler_params=_cparams("parallel", "arbitrary"),
        name=name,
    )(h, w3)


def _even_tail_kernel(w_ref, wc_ref, wf_ref, *, gate0, n_gates):
    x = w_ref[...]
    wc_ref[...] = x[:, gate0 + n_gates:].astype(BF16)
    g = x[:, gate0:gate0 + LANES]
    lane = lax.broadcasted_iota(jnp.int32, g.shape, 1)
    wf_ref[...] = jnp.where(lane < n_gates, g, 0.0).astype(BF16)


def even_tail_weights(w_in_even, gate0, n_gates, tk=256):
    n_layers, k, n = w_in_even.shape
    n_conv = n - gate0 - n_gates
    return pl.pallas_call(
        functools.partial(_even_tail_kernel, gate0=gate0, n_gates=n_gates),
        grid=(n_layers, k // tk),
        in_specs=[pl.BlockSpec((None, tk, n), lambda l, i: (l, i, 0))],
        out_specs=[pl.BlockSpec((None, tk, n_conv), lambda l, i: (l, i, 0)),
                   pl.BlockSpec((None, tk, LANES), lambda l, i: (l, i, 0))],
        out_shape=[jax.ShapeDtypeStruct((n_layers, k, n_conv), BF16),
                   jax.ShapeDtypeStruct((n_layers, k, LANES), BF16)],
        compiler_params=_cparams("parallel", "parallel"),
        name="even_tail_weights",
    )(w_in_even)


def proj_residual(lhs_list, w3, layer, res, tm=1024, tn=1024, name="proj_out"):
    m, n = res.shape
    tm, tn = _row_tile(m, tm), _row_tile(n, tn)
    k = lhs_list[0].shape[1]
    in_specs = [_lhs_spec(tm, k) for _ in lhs_list]
    in_specs += [_weight_spec(k, tn, layer, r, 0) for r in range(len(lhs_list))]
    in_specs.append(pl.BlockSpec((tm, tn), lambda j, i: (i, j)))
    return pl.pallas_call(
        functools.partial(_proj_residual_kernel, n_lhs=len(lhs_list)),
        grid=(n // tn, m // tm),
        in_specs=in_specs,
        out_specs=pl.BlockSpec((tm, tn), lambda j, i: (i, j)),
        out_shape=jax.ShapeDtypeStruct((m, n), F32),
        scratch_shapes=[pltpu.VMEM((k, tn), BF16) for _ in lhs_list],
        compiler_params=_cparams("parallel", "arbitrary"),
        name=name,
    )(*lhs_list, *([w3] * len(lhs_list)), res)


def _mlp_kernel(*refs, with_next):
    x_ref, g_ref, wu_ref, wd_ref = refs[:4]
    o_ref, h_ref = refs[4 + with_next], refs[-1]
    f = pl.program_id(1)

    @pl.when(f == 0)
    def _():
        h_ref[...] = _rms_normed(x_ref[...], g_ref[...]).astype(BF16)

    a = jnp.dot(h_ref[...], wu_ref[...], preferred_element_type=F32)
    a = jnp.maximum(a, 0.0)
    part = jnp.dot((a * a).astype(BF16), wd_ref[...], preferred_element_type=F32)

    @pl.when(f == 0)
    def _():
        o_ref[...] = x_ref[...] + part

    @pl.when(f != 0)
    def _():
        o_ref[...] += part

    if with_next:
        @pl.when(f == pl.num_programs(1) - 1)
        def _():
            refs[6][...] = _rms_normed(o_ref[...], refs[4][...]).astype(BF16)


def mlp_residual(x, gain, w_up, w_down, layer, next_gain=None, tm=512, tf=1024, x_single=False):
    m, d = x.shape
    ff = w_up.shape[2]
    tm = _row_tile(m, tm)
    with_next = next_gain is not None
    row_spec = lambda **kw: pl.BlockSpec((tm, d), lambda i, f: (i, 0), **kw)
    vec_spec = pl.BlockSpec((1, d), lambda i, f: (0, 0))
    in_specs = [row_spec(pipeline_mode=pl.Buffered(1)) if x_single else row_spec(), vec_spec,
                pl.BlockSpec((None, d, tf), lambda i, f: (layer, 0, f)),
                pl.BlockSpec((None, tf, d), lambda i, f: (layer, f, 0))]
    args = [x, gain.reshape(1, d), w_up, w_down]
    out_specs, out_shape = [row_spec()], [jax.ShapeDtypeStruct((m, d), F32)]
    if with_next:
        in_specs.append(vec_spec)
        args.append(next_gain.reshape(1, d))
        out_specs.append(row_spec())
        out_shape.append(jax.ShapeDtypeStruct((m, d), BF16))
    outs = pl.pallas_call(
        functools.partial(_mlp_kernel, with_next=int(with_next)),
        grid=(m // tm, ff // tf),
        in_specs=in_specs,
        out_specs=out_specs,
        out_shape=out_shape,
        scratch_shapes=[pltpu.VMEM((tm, d), BF16)],
        compiler_params=_cparams("parallel", "arbitrary"),
        name="mlp",
    )(*args)
    return outs if with_next else outs[0]


def _conv_kernel(cx_ref, gb_ref, gc_ref, w_ref, st_ref, y_ref, so_ref, carry_ref, *, last):
    t = pl.program_id(1)

    @pl.when(t == 0)
    def _():
        carry_ref[...] = st_ref[...]

    u = gc_ref[...] * cx_ref[...]
    rows = lax.broadcasted_iota(jnp.int32, u.shape, 0)
    prev1 = carry_ref[1:2, :]
    prev2 = carry_ref[0:1, :]
    u1 = jnp.where(rows == 0, prev1, pltpu.roll(u, 1, axis=0))
    u2 = pltpu.roll(u, 2, axis=0)
    u2 = jnp.where(rows == 0, prev2, jnp.where(rows == 1, prev1, u2))
    conv = w_ref[0:1, :] * u2 + w_ref[1:2, :] * u1 + w_ref[2:3, :] * u
    y_ref[...] = (gb_ref[...] * conv).astype(y_ref.dtype)
    tail = u[last - 2:last, :]
    carry_ref[...] = tail
    so_ref[...] = tail


def conv_gate(proj, conv_w, state, t_valid, tt_target=512):
    b, t, c3 = proj.shape
    c = c3 // 3
    tt = _row_tile(t, tt_target)
    assert t_valid == t or t == tt
    sec = lambda k: pl.BlockSpec((None, tt, c), lambda bi, ti, k=k: (bi, ti, k))
    return pl.pallas_call(
        functools.partial(_conv_kernel, last=t_valid - (t - tt)),
        grid=(b, t // tt),
        in_specs=[sec(0), sec(1), sec(2),
                  pl.BlockSpec((CONV_TAPS, c), lambda bi, ti: (0, 0)),
                  pl.BlockSpec((None, CONV_TAPS - 1, c), lambda bi, ti: (bi, 0, 0))],
        out_specs=[pl.BlockSpec((None, tt, c), lambda bi, ti: (bi, ti, 0)),
                   pl.BlockSpec((None, CONV_TAPS - 1, c), lambda bi, ti: (bi, 0, 0))],
        out_shape=[jax.ShapeDtypeStruct((b, t, c), BF16),
                   jax.ShapeDtypeStruct((b, CONV_TAPS - 1, c), F32)],
        scratch_shapes=[pltpu.VMEM((CONV_TAPS - 1, c), F32)],
        compiler_params=_cparams("parallel", "arbitrary"),
        name="conv_gate",
    )(proj, proj, proj, conv_w, state)


def _cumsum_kernel(lf_ref, kb_ref, *, chunk):
    t = lf_ref.shape[0]
    r = lax.broadcasted_iota(jnp.int32, (chunk, chunk), 0)
    c = lax.broadcasted_iota(jnp.int32, (chunk, chunk), 1)
    upper = (r <= c).astype(F32)
    carry = jnp.zeros((LANES, 1), F32)
    for i in range(t // chunk):
        lf_t = lf_ref[i * chunk:(i + 1) * chunk, :].T
        cs = jnp.dot(lf_t, upper, precision=HIGHEST, preferred_element_type=F32) + carry
        kb_ref[:, i * chunk:(i + 1) * chunk] = cs[:kb_ref.shape[0], :] * (-LOG2E)
        carry = cs[:, chunk - 1:chunk]


def fox_key_bias(logf_padded, n_heads, chunk=512):
    b, t, w = logf_padded.shape
    chunk = _row_tile(t, chunk)
    return pl.pallas_call(
        functools.partial(_cumsum_kernel, chunk=chunk),
        grid=(b,),
        in_specs=[pl.BlockSpec((None, t, w), lambda bi: (bi, 0, 0))],
        out_specs=pl.BlockSpec((None, n_heads, t), lambda bi: (bi, 0, 0)),
        out_shape=jax.ShapeDtypeStruct((b, n_heads, t), F32),
        compiler_params=_cparams("parallel"),
        name="fox_cumsum",
    )(logf_padded)


def _lane_tile(x, width):
    return x if width == LANES else jnp.concatenate([x] * (width // LANES), axis=1)


def _softmax_update(z, v, m_ref, l_ref, acc_ref, g):
    m_prev = m_ref[g]
    m_new = jnp.maximum(m_prev, jnp.max(z, axis=-1, keepdims=True))
    alpha = jnp.exp2(m_prev - m_new)
    p = jnp.exp2(z - _lane_tile(m_new, z.shape[1]))
    l_ref[g] = alpha * l_ref[g] + jnp.sum(p, axis=-1, keepdims=True)
    pv = jnp.dot(p.astype(BF16), v, preferred_element_type=F32)
    acc_ref[g] = _lane_tile(alpha, pv.shape[1]) * acc_ref[g] + pv
    m_ref[g] = m_new


def _causal_mask(z):
    r = lax.broadcasted_iota(jnp.int32, z.shape, 0)
    c = lax.broadcasted_iota(jnp.int32, z.shape, 1)
    return jnp.where(c <= r, z, MASKED)


def _scores_nt(q, k):
    return lax.dot_general(q, k.astype(BF16), (((1,), (1,)), ((), ())), preferred_element_type=F32)


def _flash_init(m_ref, l_ref, acc_ref):
    m_ref[...] = jnp.full(m_ref.shape, MASKED, F32)
    l_ref[...] = jnp.zeros(l_ref.shape, F32)
    acc_ref[...] = jnp.zeros(acc_ref.shape, F32)


def _fox_flash_kernel(qi_tab, ki_tab, q_ref, k_ref, v_ref, kb_ref, o_ref, m_ref, l_ref, acc_ref, *, hp):
    s = pl.program_id(2)
    qi, ki = qi_tab[s], ki_tab[s]

    @pl.when(ki == 0)
    def _():
        _flash_init(m_ref, l_ref, acc_ref)

    def step(masked):
        for g in range(hp):
            cols = slice(g * HEAD_DIM, (g + 1) * HEAD_DIM)
            z = _scores_nt(q_ref[:, cols], k_ref[:, cols]) + kb_ref[g]
            if masked:
                z = _causal_mask(z)
            _softmax_update(z, v_ref[:, cols], m_ref, l_ref, acc_ref, g)

    @pl.when(ki < qi)
    def _():
        step(False)

    @pl.when(ki == qi)
    def _():
        step(True)
        for g in range(hp):
            o_ref[:, g * HEAD_DIM:(g + 1) * HEAD_DIM] = (acc_ref[g] / l_ref[g]).astype(o_ref.dtype)


def _diff_flash_kernel(qi_tab, ki_tab, q_ref, k_ref, v_ref, slope_ref, lam_ref, gain_ref, o_ref,
                       m_ref, l_ref, acc_ref, *, out_scale):
    h = pl.program_id(1)
    s = pl.program_id(2)
    qi, ki = qi_tab[s], ki_tab[s]
    tq, tk = q_ref.shape[0], k_ref.shape[0]

    @pl.when(ki == 0)
    def _():
        _flash_init(m_ref, l_ref, acc_ref)

    def step(masked):
        kpos = lax.broadcasted_iota(jnp.int32, (1, tk), 1) + (ki * tk - qi * tq)
        kb = (slope_ref[h] * LOG2E) * kpos.astype(F32)
        v = v_ref[...]
        for g in range(2):
            cols = slice(g * HEAD_DIM, (g + 1) * HEAD_DIM)
            z = _scores_nt(q_ref[:, cols], k_ref[:, cols]) + kb
            if masked:
                z = _causal_mask(z)
            _softmax_update(z, v, m_ref, l_ref, acc_ref, g)

    @pl.when(ki < qi)
    def _():
        step(False)

    @pl.when(ki == qi)
    def _():
        step(True)
        dv = acc_ref.shape[2]
        o = acc_ref[0] / _lane_tile(l_ref[0], dv) - lam_ref[0] * (acc_ref[1] / _lane_tile(l_ref[1], dv))
        o_ref[...] = (_rms_normed(o, gain_ref[...]) * out_scale).astype(o_ref.dtype)


def _pair_tables(nq):
    qi = [q for q in range(nq) for _ in range(q + 1)]
    ki = [k for q in range(nq) for k in range(q + 1)]
    return jnp.asarray(qi, jnp.int32), jnp.asarray(ki, jnp.int32)


def fox_flash(q, k, v, kb, n_heads, tq=1024, hp=2):
    b, t, _ = q.shape
    tq = _row_tile(t, tq)
    w = hp * HEAD_DIM
    qi_tab, ki_tab = _pair_tables(t // tq)
    blk = lambda tab: pl.BlockSpec((None, tq, w), lambda bi, h, s, qt, kt, tab=tab: (bi, (qt, kt)[tab][s], h))
    grid_spec = pltpu.PrefetchScalarGridSpec(
        num_scalar_prefetch=2,
        grid=(b, n_heads // hp, qi_tab.shape[0]),
        in_specs=[blk(0), blk(1), blk(1),
                  pl.BlockSpec((None, hp, 1, tq), lambda bi, h, s, qt, kt: (bi, h, 0, kt[s]))],
        out_specs=blk(0),
        scratch_shapes=[pltpu.VMEM((hp, tq, LANES), F32), pltpu.VMEM((hp, tq, LANES), F32),
                        pltpu.VMEM((hp, tq, HEAD_DIM), F32)],
    )
    return pl.pallas_call(
        functools.partial(_fox_flash_kernel, hp=hp),
        grid_spec=grid_spec,
        out_shape=jax.ShapeDtypeStruct(q.shape, BF16),
        compiler_params=_cparams("parallel", "parallel", "arbitrary"),
        name="fox_flash",
    )(qi_tab, ki_tab, q, k, v, kb.reshape(b, n_heads, 1, t))


def diff_flash(q, k, v, slopes, lam, subln_gain, out_scale, n_heads, tq=1024):
    b, t, _ = q.shape
    tq = _row_tile(t, tq)
    dv = 2 * HEAD_DIM
    qi_tab, ki_tab = _pair_tables(t // tq)
    blk = lambda tab: pl.BlockSpec((None, tq, dv), lambda bi, h, s, qt, kt, tab=tab: (bi, (qt, kt)[tab][s], h))
    smem = pl.BlockSpec(memory_space=pltpu.SMEM)
    grid_spec = pltpu.PrefetchScalarGridSpec(
        num_scalar_prefetch=2,
        grid=(b, n_heads, qi_tab.shape[0]),
        in_specs=[blk(0), blk(1), blk(1), smem, smem,
                  pl.BlockSpec((1, dv), lambda bi, h, s, qt, kt: (0, 0))],
        out_specs=blk(0),
        scratch_shapes=[pltpu.VMEM((2, tq, LANES), F32), pltpu.VMEM((2, tq, LANES), F32),
                        pltpu.VMEM((2, tq, dv), F32)],
    )
    return pl.pallas_call(
        functools.partial(_diff_flash_kernel, out_scale=out_scale),
        grid_spec=grid_spec,
        out_shape=jax.ShapeDtypeStruct(q.shape, BF16),
        compiler_params=_cparams("parallel", "parallel", "arbitrary"),
        name="diff_flash",
    )(qi_tab, ki_tab, q, k, v, slopes, lam.reshape(1), subln_gain.reshape(1, dv))


def _lambda_kernel(p_ref, o_ref, *, lam_init):
    p = p_ref[...]
    s1 = jnp.sum(p[0:1] * p[1:2], axis=-1, keepdims=True)
    s2 = jnp.sum(p[2:3] * p[3:4], axis=-1, keepdims=True)
    o_ref[...] = jnp.broadcast_to(jnp.exp(s1) - jnp.exp(s2) + lam_init, o_ref.shape)


def diff_lambda(lq1, lk1, lq2, lk2, lam_init):
    p = jnp.stack([lq1, lk1, lq2, lk2]).astype(F32)
    out = pl.pallas_call(
        functools.partial(_lambda_kernel, lam_init=lam_init),
        out_shape=jax.ShapeDtypeStruct((1, LANES), F32),
        name="diff_lambda",
    )(p)
    return out[0, 0]


def _page_suffix_kernel(lf_ref, suf_ref, tot_ref):
    lf = lf_ref[...]
    r = lax.broadcasted_iota(jnp.int32, (PAGE, PAGE), 0)
    c = lax.broadcasted_iota(jnp.int32, (PAGE, PAGE), 1)
    after = (r > c).astype(F32)
    suf_ref[...] = jnp.dot(lf, after, precision=HIGHEST, preferred_element_type=F32)
    tot_ref[...] = jnp.broadcast_to(jnp.sum(lf, axis=-1, keepdims=True), lf.shape)


def page_suffix_sums(lf_rows, tr=1024):
    rows = lf_rows.shape[0]
    tr = _row_tile(rows, tr)
    spec = pl.BlockSpec((tr, PAGE), lambda i: (i, 0))
    return pl.pallas_call(
        _page_suffix_kernel,
        grid=(rows // tr,),
        in_specs=[spec],
        out_specs=[spec, spec],
        out_shape=[jax.ShapeDtypeStruct(lf_rows.shape, F32)] * 2,
        compiler_params=_cparams("parallel"),
        name="page_suffix",
    )(lf_rows)


def _head_mismatch_bias(rows, lanes, n_heads):
    r = lax.broadcasted_iota(jnp.int32, (rows, lanes), 0) % n_heads
    c = lax.broadcasted_iota(jnp.int32, (rows, lanes), 1) % n_heads
    return jnp.where(r == c, 0.0, MASKED).astype(F32)


def _decode_step(zs, v_tiles, m_ref, l_ref, acc_ref):
    m_prev = m_ref[...]
    m_new = m_prev
    for z in zs:
        m_new = jnp.maximum(m_new, jnp.max(z, axis=-1, keepdims=True))
    alpha = jnp.exp(m_prev - m_new)
    l_new = alpha * l_ref[...]
    acc = alpha * acc_ref[...]
    for z, v in zip(zs, v_tiles):
        p = jnp.exp(z - m_new)
        l_new = l_new + jnp.sum(p, axis=-1, keepdims=True)
        acc = acc + jnp.dot(p.astype(BF16), v.astype(BF16), preferred_element_type=F32)
    m_ref[...] = m_new
    l_ref[...] = l_new
    acc_ref[...] = acc


def _new_key_mask(z, n_heads, n_tok):
    r = lax.broadcasted_iota(jnp.int32, z.shape, 0)
    c = lax.broadcasted_iota(jnp.int32, z.shape, 1)
    ok = jnp.logical_and(c // n_heads <= (r // n_heads) % n_tok, c % n_heads == r % n_heads)
    return jnp.where(ok, z, MASKED)


def _decode_fox_kernel(pt_ref, q_ref, *refs, pps, n_heads, n_tok):
    k_refs, v_refs = refs[:pps], refs[pps:2 * pps]
    suf_refs, tot_refs = refs[2 * pps:3 * pps], refs[3 * pps:4 * pps]
    knew_ref, vnew_ref, lfnew_ref, o_ref, m_ref, l_ref, acc_ref, carry_ref = refs[4 * pps:]
    c = pl.program_id(1)

    @pl.when(c == 0)
    def _():
        _flash_init(m_ref, l_ref, acc_ref)
        carry_ref[...] = jnp.zeros(carry_ref.shape, F32)

    q = q_ref[...]
    mism = _head_mismatch_bias(q.shape[0], PAGE * n_heads, n_heads)
    carry = carry_ref[...]
    zs = [None] * pps
    for j in reversed(range(pps)):
        zs[j] = _scores_nt(q, k_refs[j][...]) + (suf_refs[j][...] + carry + mism)
        carry = carry + tot_refs[j][...]
    carry_ref[...] = carry
    _decode_step(zs, [v[...] for v in v_refs], m_ref, l_ref, acc_ref)

    @pl.when(c == pl.num_programs(1) - 1)
    def _():
        lf = lfnew_ref[...]
        cum = lf
        for i in range(1, n_tok):
            cum = cum + pltpu.roll(lf, i * n_heads, axis=1)
        z = _scores_nt(q, knew_ref[...]) - cum
        _decode_step([_new_key_mask(z, n_heads, n_tok)], [vnew_ref[...]], m_ref, l_ref, acc_ref)
        o_ref[...] = acc_ref[...] / l_ref[...]


def _decode_diff_kernel(pt_ref, q_ref, *refs, pps, n_heads, n_tok, past_len, out_scale):
    k_refs, v_refs = refs[:pps], refs[pps:2 * pps]
    (knew_ref, vnew_ref, slope_ref, lam_ref, gain_ref, o_ref, m_ref, l_ref, acc_ref) = refs[2 * pps:]
    c = pl.program_id(1)
    n_chunks = pl.num_programs(1)
    half = n_tok * n_heads
    rows_per_page = PAGE * n_heads

    @pl.when(c == 0)
    def _():
        _flash_init(m_ref, l_ref, acc_ref)

    q = q_ref[...]
    slope = slope_ref[...]
    key = lax.broadcasted_iota(jnp.int32, (1, rows_per_page), 1) // n_heads
    mism = _head_mismatch_bias(2 * half, rows_per_page, n_heads)
    zs = []
    for j in range(pps):
        first = ((n_chunks - 1 - c) * pps + j) * PAGE
        bias = slope * (key + (first - past_len)).astype(F32) + mism
        z = [_scores_nt(q[g * half:(g + 1) * half], k_refs[j][pl.ds(g, rows_per_page, stride=2), :])
             for g in range(2)]
        zs.append(jnp.concatenate(z, axis=0) + bias)
    _decode_step(zs, [v[...] for v in v_refs], m_ref, l_ref, acc_ref)

    @pl.when(c == n_chunks - 1)
    def _():
        z = [_scores_nt(q[g * half:(g + 1) * half], knew_ref[g]) for g in range(2)]
        lane = lax.broadcasted_iota(jnp.int32, (1, PAGE), 1)
        z = jnp.concatenate(z, axis=0) + slope[:, :PAGE] * (lane // n_heads).astype(F32)
        _decode_step([_new_key_mask(z, n_heads, n_tok)], [vnew_ref[...]], m_ref, l_ref, acc_ref)
        o = acc_ref[...] / l_ref[...]
        o = o[:half] - lam_ref[0] * o[half:]
        o_ref[...] = _rms_normed(o, gain_ref[...]) * out_scale


def _paged_specs(block, layer, n_pages, pps):
    def spec(j):
        def imap(bi, c, pt):
            return (layer, pt[bi, n_pages - (c + 1) * pps + j], 0, 0)
        return pl.BlockSpec((None, None) + block, imap)
    return [spec(j) for j in range(pps)]


def decode_fox(q, k_pool, v_pool, suf_pool, tot_pool, layer, page_table, k_new, v_new, lf_new, n_heads, n_tok, pps=8):
    b, rows, _ = q.shape
    n_pages = page_table.shape[1]
    page_rows = PAGE * n_heads
    per_b = lambda shp: pl.BlockSpec((None,) + shp, lambda bi, c, pt: (bi, 0, 0))
    grid_spec = pltpu.PrefetchScalarGridSpec(
        num_scalar_prefetch=1,
        grid=(b, n_pages // pps),
        in_specs=([per_b((rows, HEAD_DIM))]
                  + _paged_specs((page_rows, HEAD_DIM), layer, n_pages, pps) * 2
                  + _paged_specs((1, page_rows), 0, n_pages, pps) * 2
                  + [per_b((PAGE, HEAD_DIM)), per_b((PAGE, HEAD_DIM)), per_b((1, PAGE))]),
        out_specs=per_b((rows, HEAD_DIM)),
        scratch_shapes=[pltpu.VMEM((rows, 1), F32), pltpu.VMEM((rows, 1), F32),
                        pltpu.VMEM((rows, HEAD_DIM), F32), pltpu.VMEM((1, page_rows), F32)],
    )
    return pl.pallas_call(
        functools.partial(_decode_fox_kernel, pps=pps, n_heads=n_heads, n_tok=n_tok),
        grid_spec=grid_spec,
        out_shape=jax.ShapeDtypeStruct((b, rows, HEAD_DIM), F32),
        compiler_params=_cparams("parallel", "arbitrary"),
        name="decode_fox",
    )(page_table, q, *([k_pool] * pps), *([v_pool] * pps), *([suf_pool] * pps), *([tot_pool] * pps),
      k_new, v_new, lf_new)


def decode_diff(q, k_pool, v_pool, layer, page_table, k_new, v_new, slope_lanes, lam, subln_gain, out_scale,
                n_heads, n_tok, pps=4):
    b, rows, _ = q.shape
    dv = 2 * HEAD_DIM
    n_pages = page_table.shape[1]
    page_rows = PAGE * n_heads
    per_b = lambda shp: pl.BlockSpec((None,) + shp, lambda bi, c, pt: (bi,) + (0,) * len(shp))
    const = lambda shp: pl.BlockSpec(shp, lambda bi, c, pt: (0, 0))
    grid_spec = pltpu.PrefetchScalarGridSpec(
        num_scalar_prefetch=1,
        grid=(b, n_pages // pps),
        in_specs=([per_b((rows, HEAD_DIM))]
                  + _paged_specs((2 * page_rows, HEAD_DIM), layer, n_pages, pps)
                  + _paged_specs((page_rows, dv), layer, n_pages, pps)
                  + [per_b((2, PAGE, HEAD_DIM)), per_b((PAGE, dv)), const((1, page_rows)),
                     pl.BlockSpec(memory_space=pltpu.SMEM), const((1, dv))]),
        out_specs=per_b((rows // 2, dv)),
        scratch_shapes=[pltpu.VMEM((rows, 1), F32), pltpu.VMEM((rows, 1), F32),
                        pltpu.VMEM((rows, dv), F32)],
    )
    return pl.pallas_call(
        functools.partial(_decode_diff_kernel, pps=pps, n_heads=n_heads, n_tok=n_tok,
                          past_len=n_pages * PAGE, out_scale=out_scale),
        grid_spec=grid_spec,
        out_shape=jax.ShapeDtypeStruct((b, rows // 2, dv), F32),
        compiler_params=_cparams("parallel", "arbitrary"),
        name="decode_diff",
    )(page_table, q, *([k_pool] * pps), *([v_pool] * pps), k_new, v_new,
      slope_lanes, lam.reshape(1), subln_gain.reshape(1, dv))


MLP_TILES = ((512, 1024, False), (1024, 512, True), (512, 2048, True), (1024, 1024, True))
FOX_HEADS_PER_STEP = (2, 4)
FOX_PAGES_PER_STEP = (8, 16)
DIFF_PAGES_PER_STEP = (4, 8)


def _pad_rows(x, rows):
    pad = [(0, 0)] * x.ndim
    pad[-2] = (0, rows - x.shape[-2])
    return jnp.pad(x, pad)


def kernel(x_prompt, x_sample, cache_fox_k, cache_fox_v, cache_fox_logf, state_conv, cache_diff_k, cache_diff_v,
           page_table, attn_norm_gain, mlp_norm_gain, w_in_even, w_out_even, fox_q_gain, fox_k_gain, fox_f_bias,
           conv_w, w_in_odd, w_out_odd, diff_q_gain, diff_k_gain, diff_lq1, diff_lk1, diff_lq2, diff_lk2,
           diff_subln_gain, w_up, w_down):
    bp, tp, d = x_prompt.shape
    bs, ts, _ = x_sample.shape
    depth = attn_norm_gain.shape[0]
    n_fox_layers, n_diff_layers = w_in_even.shape[0], w_in_odd.shape[0]
    n_fox = cache_fox_k.shape[3]
    fox_w = n_fox * HEAD_DIM
    conv_c = state_conv.shape[-1]
    n_diff = cache_diff_k.shape[3]
    qk_w = n_diff * 2 * HEAD_DIM
    dv = cache_diff_v.shape[-1]
    v_w = n_diff * dv
    n_phys = cache_fox_k.shape[1]
    diff_v_rows = tuple((c % 2) * n_diff + c // 2 for c in range(v_w // HEAD_DIM))

    fox_k_pool = cache_fox_k.reshape(-1, n_phys, PAGE * n_fox, HEAD_DIM)
    fox_v_pool = cache_fox_v.reshape(-1, n_phys, PAGE * n_fox, HEAD_DIM)
    diff_k_pool = cache_diff_k.reshape(-1, n_phys, PAGE * n_diff * 2, HEAD_DIM)
    diff_v_pool = cache_diff_v.reshape(-1, n_phys, PAGE * n_diff, dv)
    slopes = jnp.asarray([2.0 ** (-8.0 * (h + 1) / n_diff) for h in range(n_diff)], F32)
    slope_lanes = jnp.tile(slopes, PAGE).reshape(1, PAGE * n_diff)

    xp = x_prompt.reshape(bp * tp, d)
    xs = x_sample.reshape(bs * ts, d)
    mp, ms = bp * tp, bs * ts
    ts_pad = SUBLANES
    leaf = {k: None for k in ("fk_p", "fv_p", "dk_p", "dv_p", "fk_s", "fv_s", "dk_s", "dv_s")}
    small = {k: [] for k in ("fl_p", "cs_p", "fl_s", "cs_s")}

    wc3, wf3 = even_tail_weights(w_in_even, 3 * fox_w, n_fox)
    w_up_bf, w_down_bf = w_up.astype(BF16), w_down.astype(BF16)
    hp = rmsnorm_bf16(xp, attn_norm_gain[0])
    hs = rmsnorm_bf16(xs, attn_norm_gain[0])

    for i in range(depth):
        j = i // 2
        if i % 2 == 0:
            bias = jnp.pad(fox_f_bias[j], (0, LANES - n_fox))
            lf_rows = jnp.swapaxes(cache_fox_logf[j], 1, 2).reshape(n_phys * n_fox, PAGE)
            suf, tot = page_suffix_sums(lf_rows)
            to_lanes = lambda a: jnp.swapaxes(a.reshape(n_phys, n_fox, PAGE), 1, 2).reshape(1, n_phys, 1, PAGE * n_fox)
            suf_pool, tot_pool = to_lanes(suf), to_lanes(tot)
            results = []
            for h_act, x_res, sample in ((hp, xp, False), (hs, xs, True)):
                tag = "_s" if sample else "_p"
                q_scale = QK_SCALE if sample else QK_SCALE * LOG2E
                qn = proj_heads(h_act, w_in_even, j, 0, fox_w, gain=fox_q_gain[j], scale=q_scale, name="fox_q")
                kn, leaf["fk" + tag] = proj_heads(h_act, w_in_even, j, fox_w, fox_w, gain=fox_k_gain[j],
                                                  leaf=(n_fox_layers, n_fox), leaf_prev=leaf["fk" + tag], name="fox_k")
                vv, leaf["fv" + tag] = proj_heads(h_act, w_in_even, j, 2 * fox_w, fox_w,
                                                  leaf=(n_fox_layers, n_fox), leaf_prev=leaf["fv" + tag], name="fox_v")
                lf = proj_logsigmoid(h_act, wf3, j, bias)
                cproj = proj_f32(h_act, wc3, j, name="conv_proj")
                if not sample:
                    kb = fox_key_bias(lf.reshape(bp, tp, LANES), n_fox)
                    attn = fox_flash(qn.reshape(bp, tp, fox_w), kn.reshape(bp, tp, fox_w),
                                     vv.reshape(bp, tp, fox_w), kb, n_fox, hp=FOX_HEADS_PER_STEP[j]).reshape(mp, fox_w)
                    yc, cs = conv_gate(cproj.reshape(bp, tp, 3 * conv_c), conv_w[j],
                                       jnp.zeros((bp, CONV_TAPS - 1, conv_c), F32), tp)
                    yc = yc.reshape(mp, conv_c)
                    small["fl_p"].append(lf.reshape(bp, tp, LANES)[..., :n_fox])
                    small["cs_p"].append(cs)
                else:
                    lf_s = lf.reshape(bs, ts, LANES)[..., :n_fox]
                    lf_new = _pad_rows(lf_s.reshape(bs, ts * n_fox, 1), PAGE).reshape(bs, 1, PAGE)
                    new_rows = lambda lv: _pad_rows(lv[j].reshape(bs, ts * n_fox, HEAD_DIM), PAGE)
                    attn = decode_fox(qn.reshape(bs, ts * n_fox, HEAD_DIM), fox_k_pool, fox_v_pool, suf_pool, tot_pool,
                                      j, page_table, new_rows(leaf["fk_s"]), new_rows(leaf["fv_s"]), lf_new, n_fox, ts,
                                      pps=FOX_PAGES_PER_STEP[j])
                    attn = attn.reshape(ms, fox_w).astype(BF16)
                    cpad = _pad_rows(cproj.reshape(bs, ts, 3 * conv_c), ts_pad)
                    yc, cs = conv_gate(cpad, conv_w[j], state_conv[j], ts)
                    yc = yc[:, :ts].reshape(ms, conv_c)
                    small["fl_s"].append(lf_s)
                    small["cs_s"].append(cs)
                results.append(proj_residual([attn, yc], w_out_even, j, x_res, name="even_out"))
            xp, xs = results
        else:
            lam_init = 0.8 - 0.6 * math.exp(-0.3 * i)
            lam = diff_lambda(diff_lq1[j], diff_lk1[j], diff_lq2[j], diff_lk2[j], lam_init)
            results = []
            for h_act, x_res, sample in ((hp, xp, False), (hs, xs, True)):
                tag = "_s" if sample else "_p"
                q_scale = QK_SCALE if sample else QK_SCALE * LOG2E
                qn = proj_heads(h_act, w_in_odd, j, 0, qk_w, gain=diff_q_gain[j], scale=q_scale, name="diff_q")
                kn, leaf["dk" + tag] = proj_heads(h_act, w_in_odd, j, qk_w, qk_w, gain=diff_k_gain[j],
                                                  leaf=(n_diff_layers, qk_w // HEAD_DIM), leaf_prev=leaf["dk" + tag],
                                                  name="diff_k")
                vv, leaf["dv" + tag] = proj_heads(h_act, w_in_odd, j, 2 * qk_w, v_w,
                                                  leaf=(n_diff_layers, v_w // HEAD_DIM), leaf_prev=leaf["dv" + tag],
                                                  leaf_rows=diff_v_rows, tm=512, tn=v_w, name="diff_v")
                if not sample:
                    o = diff_flash(qn.reshape(bp, tp, qk_w), kn.reshape(bp, tp, qk_w), vv.reshape(bp, tp, v_w),
                                   slopes, lam, diff_subln_gain[j], 1.0 - lam_init, n_diff)
                    o = o.reshape(mp, v_w)
                else:
                    split = lambda a: jnp.moveaxis(a.reshape(bs, ts, n_diff, 2, HEAD_DIM), 3, 1)
                    q_rows = split(qn).reshape(bs, 2 * ts * n_diff, HEAD_DIM)
                    k_new = _pad_rows(split(leaf["dk_s"][j]).reshape(bs, 2, ts * n_diff, HEAD_DIM), PAGE)
                    v_new = jnp.swapaxes(leaf["dv_s"][j].reshape(bs, ts, 2, n_diff, HEAD_DIM), 2, 3)
                    v_new = _pad_rows(v_new.reshape(bs, ts * n_diff, dv), PAGE)
                    o = decode_diff(q_rows, diff_k_pool, diff_v_pool, j, page_table, k_new, v_new,
                                    slope_lanes, lam, diff_subln_gain[j], 1.0 - lam_init, n_diff, ts,
                                    pps=DIFF_PAGES_PER_STEP[j])
                    o = o.reshape(ms, v_w).astype(BF16)
                results.append(proj_residual([o], w_out_odd, j, x_res, name="odd_out"))
            xp, xs = results
        tm, tf, x_single = MLP_TILES[i]
        if i + 1 < depth:
            xp, hp = mlp_residual(xp, mlp_norm_gain[i], w_up_bf, w_down_bf, i, next_gain=attn_norm_gain[i + 1],
                                  tm=tm, tf=tf, x_single=x_single)
            xs, hs = mlp_residual(xs, mlp_norm_gain[i], w_up_bf, w_down_bf, i, next_gain=attn_norm_gain[i + 1])
        else:
            xp = mlp_residual(xp, mlp_norm_gain[i], w_up_bf, w_down_bf, i, tm=tm, tf=tf, x_single=x_single)
            xs = mlp_residual(xs, mlp_norm_gain[i], w_up_bf, w_down_bf, i)

    st = {k: jnp.stack(v) for k, v in small.items()}
    fox_leaf = lambda a, b, t: a.reshape(n_fox_layers, b, t, n_fox, HEAD_DIM)
    dk_leaf = lambda a, b, t: a.reshape(n_diff_layers, b, t, n_diff, 2, HEAD_DIM)
    dv_leaf = lambda a, b, t: jnp.swapaxes(a.reshape(n_diff_layers, b, t, 2, n_diff, HEAD_DIM), 3, 4).reshape(
        n_diff_layers, b, t, n_diff, dv)
    return (xp.reshape(bp, tp, d), xs.reshape(bs, ts, d),
            fox_leaf(leaf["fk_p"], bp, tp), fox_leaf(leaf["fv_p"], bp, tp), st["fl_p"], st["cs_p"],
            dk_leaf(leaf["dk_p"], bp, tp), dv_leaf(leaf["dv_p"], bp, tp),
            fox_leaf(leaf["fk_s"], bs, ts), fox_leaf(leaf["fv_s"], bs, ts), st["fl_s"], st["cs_s"],
            dk_leaf(leaf["dk_s"], bs, ts), dv_leaf(leaf["dv_s"], bs, ts))
```

```python
import functools
import math

import jax
import jax.numpy as jnp
from jax import lax
from jax.experimental import pallas as pl
from jax.experimental.pallas import tpu as pltpu

F32 = jnp.float32
BF16 = jnp.bfloat16

HEAD_DIM = 128
LANES = 128
SUBLANES = 8
PAGE = 128
CONV_TAPS = 3
NORM_EPS = 1e-6
MASKED = -1e30
QK_SCALE = HEAD_DIM ** -0.5
LOG2E = math.log2(math.e)
VMEM_LIMIT = 56 * 1024 * 1024
HIGHEST = lax.Precision.HIGHEST


def _cparams(*sem):
    return pltpu.CompilerParams(dimension_semantics=sem, vmem_limit_bytes=VMEM_LIMIT)


def _row_tile(m, target):
    return m if m <= target else target


def _rms_normed(x, gain):
    ms = jnp.mean(x * x, axis=-1, keepdims=True)
    return x * lax.rsqrt(ms + NORM_EPS) * gain


def _rmsnorm_kernel(x_ref, g_ref, o_ref):
    o_ref[...] = _rms_normed(x_ref[...], g_ref[...]).astype(o_ref.dtype)


def rmsnorm_bf16(x, gain):
    m, d = x.shape
    tm = _row_tile(m, 512)
    return pl.pallas_call(
        _rmsnorm_kernel,
        grid=(m // tm,),
        in_specs=[pl.BlockSpec((tm, d), lambda i: (i, 0)),
                  pl.BlockSpec((1, d), lambda i: (0, 0))],
        out_specs=pl.BlockSpec((tm, d), lambda i: (i, 0)),
        out_shape=jax.ShapeDtypeStruct((m, d), BF16),
        compiler_params=_cparams("parallel"),
        name="rmsnorm",
    )(x, gain.reshape(1, d))


def _cached_bf16(w_ref, wb_ref):
    @pl.when(pl.program_id(1) == 0)
    def _():
        wb_ref[...] = w_ref[...].astype(BF16)


def _proj_headnorm_kernel(*refs, scale, leaf, alias):
    h_ref, w_ref, g_ref = refs[:3]
    outs = refs[3 + alias:]
    o_ref, wb_ref = outs[0], outs[-1]
    _cached_bf16(w_ref, wb_ref)
    acc = jnp.dot(h_ref[...], wb_ref[...], preferred_element_type=F32)
    g = g_ref[...]
    for hh in range(acc.shape[1] // HEAD_DIM):
        cols = slice(hh * HEAD_DIM, (hh + 1) * HEAD_DIM)
        blk = acc[:, cols]
        y = blk * lax.rsqrt(jnp.mean(blk * blk, axis=-1, keepdims=True) + NORM_EPS) * g
        if leaf:
            outs[1][:, hh, :] = y
        o_ref[:, cols] = (y * scale).astype(o_ref.dtype)


def _proj_plain_kernel(*refs, leaf_rows, alias):
    h_ref, w_ref = refs[:2]
    outs = refs[2 + alias:]
    o_ref, wb_ref = outs[0], outs[-1]
    _cached_bf16(w_ref, wb_ref)
    acc = jnp.dot(h_ref[...], wb_ref[...], preferred_element_type=F32)
    o_ref[...] = acc.astype(o_ref.dtype)
    if leaf_rows is not None:
        for cc, row in enumerate(leaf_rows):
            outs[1][:, row, :] = acc[:, cc * HEAD_DIM:(cc + 1) * HEAD_DIM]


def _proj_logsigmoid_kernel(h_ref, w_ref, b_ref, o_ref, wb_ref):
    _cached_bf16(w_ref, wb_ref)
    z = jnp.dot(h_ref[...], wb_ref[...], preferred_element_type=F32) + b_ref[...]
    o_ref[...] = jnp.minimum(z, 0.0) - jnp.log1p(jnp.exp(-jnp.abs(z)))


def _proj_residual_kernel(*refs, n_lhs):
    a_refs, w_refs = refs[:n_lhs], refs[n_lhs:2 * n_lhs]
    r_ref, o_ref = refs[2 * n_lhs], refs[2 * n_lhs + 1]
    wb_refs = refs[2 * n_lhs + 2:]
    acc = r_ref[...]
    for a_ref, w_ref, wb_ref in zip(a_refs, w_refs, wb_refs):
        _cached_bf16(w_ref, wb_ref)
        acc = acc + jnp.dot(a_ref[...], wb_ref[...], preferred_element_type=F32)
    o_ref[...] = acc


def _weight_spec(k_rows, tn, layer, row_blk, col_blk0):
    return pl.BlockSpec((None, k_rows, tn), lambda j, i: (layer, row_blk, col_blk0 + j),
                        pipeline_mode=pl.Buffered(1))


def _lhs_spec(tm, k):
    return pl.BlockSpec((tm, k), lambda j, i: (i, 0))


def _leaf_args(prev, n_layers, m, groups, layer, tm, chunks):
    spec = pl.BlockSpec((None, tm, chunks, LANES), lambda j, i: (layer, i, j, 0))
    shape = jax.ShapeDtypeStruct((n_layers, m, groups, LANES), F32)
    return spec, shape


def proj_heads(h, w3, layer, col0, n_cols, gain=None, scale=1.0, leaf=None, leaf_prev=None, leaf_rows=None,
               tm=1024, tn=1024, name="proj"):
    m, k = h.shape
    tm, tn = _row_tile(m, tm), _row_tile(n_cols, tn)
    chunks = tn // HEAD_DIM
    alias = leaf_prev is not None
    in_specs = [_lhs_spec(tm, k), _weight_spec(k, tn, layer, 0, col0 // tn)]
    args = [h, w3]
    if gain is not None:
        in_specs.append(pl.BlockSpec((1, HEAD_DIM), lambda j, i: (0, 0)))
        args.append(gain.reshape(1, HEAD_DIM).astype(F32))
    out_specs = [pl.BlockSpec((tm, tn), lambda j, i: (i, j))]
    out_shape = [jax.ShapeDtypeStruct((m, n_cols), BF16)]
    aliases = {}
    if leaf is not None:
        n_layers, groups = leaf
        rows_per_tile = chunks if leaf_rows is None else groups
        spec, shape = _leaf_args(leaf_prev, n_layers, m, groups, layer, tm, rows_per_tile)
        out_specs.append(spec)
        out_shape.append(shape)
        if alias:
            in_specs.append(pl.BlockSpec(memory_space=pl.ANY))
            args.append(leaf_prev)
            aliases = {len(args) - 1: 1}
    if gain is not None:
        body = functools.partial(_proj_headnorm_kernel, scale=scale, leaf=leaf is not None, alias=int(alias))
    else:
        rows = None if leaf is None else (leaf_rows if leaf_rows is not None else tuple(range(chunks)))
        body = functools.partial(_proj_plain_kernel, leaf_rows=rows, alias=int(alias))
    outs = pl.pallas_call(
        body,
        grid=(n_cols // tn, m // tm),
        in_specs=in_specs,
        out_specs=out_specs,
        out_shape=out_shape,
        scratch_shapes=[pltpu.VMEM((k, tn), BF16)],
        input_output_aliases=aliases,
        compiler_params=_cparams("parallel", "arbitrary"),
        name=name,
    )(*args)
    return outs if leaf is not None else outs[0]


def proj_logsigmoid(h, w3, layer, bias, name="proj_logf"):
    m, k = h.shape
    n = w3.shape[2]
    tm = _row_tile(m, 1024)
    return pl.pallas_call(
        _proj_logsigmoid_kernel,
        grid=(1, m // tm),
        in_specs=[_lhs_spec(tm, k), _weight_spec(k, n, layer, 0, 0), pl.BlockSpec((1, n), lambda j, i: (0, 0))],
        out_specs=pl.BlockSpec((tm, n), lambda j, i: (i, 0)),
        out_shape=jax.ShapeDtypeStruct((m, n), F32),
        scratch_shapes=[pltpu.VMEM((k, n), BF16)],
        compiler_params=_cparams("parallel", "arbitrary"),
        name=name,
    )(h, w3, bias.reshape(1, n))


def proj_f32(h, w3, layer, tm=1024, tn=1024, name="proj_f32"):
    m, k = h.shape
    n = w3.shape[2]
    tm, tn = _row_tile(m, tm), _row_tile(n, tn)
    return pl.pallas_call(
        functools.partial(_proj_plain_kernel, leaf_rows=None, alias=0),
        grid=(n // tn, m // tm),
        in_specs=[_lhs_spec(tm, k), _weight_spec(k, tn, layer, 0, 0)],
        out_specs=pl.BlockSpec((tm, tn), lambda j, i: (i, j)),
        out_shape=jax.ShapeDtypeStruct((m, n), F32),
        scratch_shapes=[pltpu.VMEM((k, tn), BF16)],
        compiler_params=_cparams("parallel", "arbitrary"),
        name=name,
    )(h, w3)


def _even_tail_kernel(w_ref, wc_ref, wf_ref, *, gate0, n_gates):
    x = w_ref[...]
    wc_ref[...] = x[:, gate0 + n_gates:].astype(BF16)
    g = x[:, gate0:gate0 + LANES]
    lane = lax.broadcasted_iota(jnp.int32, g.shape, 1)
    wf_ref[...] = jnp.where(lane < n_gates, g, 0.0).astype(BF16)


def even_tail_weights(w_in_even, gate0, n_gates, tk=256):
    n_layers, k, n = w_in_even.shape
    n_conv = n - gate0 - n_gates
    return pl.pallas_call(
        functools.partial(_even_tail_kernel, gate0=gate0, n_gates=n_gates),
        grid=(n_layers, k // tk),
        in_specs=[pl.BlockSpec((None, tk, n), lambda l, i: (l, i, 0))],
        out_specs=[pl.BlockSpec((None, tk, n_conv), lambda l, i: (l, i, 0)),
                   pl.BlockSpec((None, tk, LANES), lambda l, i: (l, i, 0))],
        out_shape=[jax.ShapeDtypeStruct((n_layers, k, n_conv), BF16),
                   jax.ShapeDtypeStruct((n_layers, k, LANES), BF16)],
        compiler_params=_cparams("parallel", "parallel"),
        name="even_tail_weights",
    )(w_in_even)


def proj_residual(lhs_list, w3, layer, res, tm=1024, tn=1024, name="proj_out"):
    m, n = res.shape
    tm, tn = _row_tile(m, tm), _row_tile(n, tn)
    k = lhs_list[0].shape[1]
    in_specs = [_lhs_spec(tm, k) for _ in lhs_list]
    in_specs += [_weight_spec(k, tn, layer, r, 0) for r in range(len(lhs_list))]
    in_specs.append(pl.BlockSpec((tm, tn), lambda j, i: (i, j)))
    return pl.pallas_call(
        functools.partial(_proj_residual_kernel, n_lhs=len(lhs_list)),
        grid=(n // tn, m // tm),
        in_specs=in_specs,
        out_specs=pl.BlockSpec((tm, tn), lambda j, i: (i, j)),
        out_shape=jax.ShapeDtypeStruct((m, n), F32),
        scratch_shapes=[pltpu.VMEM((k, tn), BF16) for _ in lhs_list],
        compiler_params=_cparams("parallel", "arbitrary"),
        name=name,
    )(*lhs_list, *([w3] * len(lhs_list)), res)


def _mlp_kernel(*refs, with_next, n_split):
    x_ref, g_ref, wu_ref, wd_ref = refs[:4]
    o_ref, h_ref = refs[4 + with_next], refs[-1]
    f = pl.program_id(1)

    @pl.when(f == 0)
    def _():
        h_ref[...] = _rms_normed(x_ref[...], g_ref[...]).astype(BF16)

    part = None
    tfs = wu_ref.shape[1] // n_split
    for c in range(n_split):
        a = jnp.dot(h_ref[...], wu_ref[:, c * tfs:(c + 1) * tfs], preferred_element_type=F32)
        a = jnp.maximum(a, 0.0)
        p = jnp.dot((a * a).astype(BF16), wd_ref[c * tfs:(c + 1) * tfs, :], preferred_element_type=F32)
        part = p if part is None else part + p

    @pl.when(f == 0)
    def _():
        o_ref[...] = x_ref[...] + part

    @pl.when(f != 0)
    def _():
        o_ref[...] += part

    if with_next:
        @pl.when(f == pl.num_programs(1) - 1)
        def _():
            refs[6][...] = _rms_normed(o_ref[...], refs[4][...]).astype(BF16)


def mlp_residual(x, gain, w_up, w_down, layer, next_gain=None, tm=512, tf=1024, x_single=False, n_split=1):
    m, d = x.shape
    ff = w_up.shape[2]
    tm = _row_tile(m, tm)
    with_next = next_gain is not None
    row_spec = lambda **kw: pl.BlockSpec((tm, d), lambda i, f: (i, 0), **kw)
    vec_spec = pl.BlockSpec((1, d), lambda i, f: (0, 0))
    in_specs = [row_spec(pipeline_mode=pl.Buffered(1)) if x_single else row_spec(), vec_spec,
                pl.BlockSpec((None, d, tf), lambda i, f: (layer, 0, f)),
                pl.BlockSpec((None, tf, d), lambda i, f: (layer, f, 0))]
    args = [x, gain.reshape(1, d), w_up, w_down]
    out_specs, out_shape = [row_spec()], [jax.ShapeDtypeStruct((m, d), F32)]
    if with_next:
        in_specs.append(vec_spec)
        args.append(next_gain.reshape(1, d))
        out_specs.append(row_spec())
        out_shape.append(jax.ShapeDtypeStruct((m, d), BF16))
    outs = pl.pallas_call(
        functools.partial(_mlp_kernel, with_next=int(with_next), n_split=n_split),
        grid=(m // tm, ff // tf),
        in_specs=in_specs,
        out_specs=out_specs,
        out_shape=out_shape,
        scratch_shapes=[pltpu.VMEM((tm, d), BF16)],
        compiler_params=_cparams("parallel", "arbitrary"),
        name="mlp",
    )(*args)
    return outs if with_next else outs[0]


def _conv_kernel(cx_ref, gb_ref, gc_ref, w_ref, st_ref, y_ref, so_ref, carry_ref, *, last):
    t = pl.program_id(1)

    @pl.when(t == 0)
    def _():
        carry_ref[...] = st_ref[...]

    u = gc_ref[...] * cx_ref[...]
    rows = lax.broadcasted_iota(jnp.int32, u.shape, 0)
    prev1 = carry_ref[1:2, :]
    prev2 = carry_ref[0:1, :]
    u1 = jnp.where(rows == 0, prev1, pltpu.roll(u, 1, axis=0))
    u2 = pltpu.roll(u, 2, axis=0)
    u2 = jnp.where(rows == 0, prev2, jnp.where(rows == 1, prev1, u2))
    conv = w_ref[0:1, :] * u2 + w_ref[1:2, :] * u1 + w_ref[2:3, :] * u
    y_ref[...] = (gb_ref[...] * conv).astype(y_ref.dtype)
    tail = u[last - 2:last, :]
    carry_ref[...] = tail
    so_ref[...] = tail


def conv_gate(proj, conv_w, state, t_valid, tt_target=512):
    b, t, c3 = proj.shape
    c = c3 // 3
    tt = _row_tile(t, tt_target)
    assert t_valid == t or t == tt
    sec = lambda k: pl.BlockSpec((None, tt, c), lambda bi, ti, k=k: (bi, ti, k))
    return pl.pallas_call(
        functools.partial(_conv_kernel, last=t_valid - (t - tt)),
        grid=(b, t // tt),
        in_specs=[sec(0), sec(1), sec(2),
                  pl.BlockSpec((CONV_TAPS, c), lambda bi, ti: (0, 0)),
                  pl.BlockSpec((None, CONV_TAPS - 1, c), lambda bi, ti: (bi, 0, 0))],
        out_specs=[pl.BlockSpec((None, tt, c), lambda bi, ti: (bi, ti, 0)),
                   pl.BlockSpec((None, CONV_TAPS - 1, c), lambda bi, ti: (bi, 0, 0))],
        out_shape=[jax.ShapeDtypeStruct((b, t, c), BF16),
                   jax.ShapeDtypeStruct((b, CONV_TAPS - 1, c), F32)],
        scratch_shapes=[pltpu.VMEM((CONV_TAPS - 1, c), F32)],
        compiler_params=_cparams("parallel", "arbitrary"),
        name="conv_gate",
    )(proj, proj, proj, conv_w, state)


def _cumsum_kernel(lf_ref, kb_ref, *, chunk):
    t = lf_ref.shape[0]
    r = lax.broadcasted_iota(jnp.int32, (chunk, chunk), 0)
    c = lax.broadcasted_iota(jnp.int32, (chunk, chunk), 1)
    upper = (r <= c).astype(F32)
    carry = jnp.zeros((LANES, 1), F32)
    for i in range(t // chunk):
        lf_t = lf_ref[i * chunk:(i + 1) * chunk, :].T
        cs = jnp.dot(lf_t, upper, precision=HIGHEST, preferred_element_type=F32) + carry
        kb_ref[:, i * chunk:(i + 1) * chunk] = cs[:kb_ref.shape[0], :] * (-LOG2E)
        carry = cs[:, chunk - 1:chunk]


def fox_key_bias(logf_padded, n_heads, chunk=512):
    b, t, w = logf_padded.shape
    chunk = _row_tile(t, chunk)
    return pl.pallas_call(
        functools.partial(_cumsum_kernel, chunk=chunk),
        grid=(b,),
        in_specs=[pl.BlockSpec((None, t, w), lambda bi: (bi, 0, 0))],
        out_specs=pl.BlockSpec((None, n_heads, t), lambda bi: (bi, 0, 0)),
        out_shape=jax.ShapeDtypeStruct((b, n_heads, t), F32),
        compiler_params=_cparams("parallel"),
        name="fox_cumsum",
    )(logf_padded)


def _lane_tile(x, width):
    return x if width == LANES else jnp.concatenate([x] * (width // LANES), axis=1)


def _softmax_update(z, v, m_ref, l_ref, acc_ref, g):
    m_prev = m_ref[g]
    m_new = jnp.maximum(m_prev, jnp.max(z, axis=-1, keepdims=True))
    alpha = jnp.exp2(m_prev - m_new)
    p = jnp.exp2(z - _lane_tile(m_new, z.shape[1]))
    l_ref[g] = alpha * l_ref[g] + jnp.sum(p, axis=-1, keepdims=True)
    pv = jnp.dot(p.astype(BF16), v, preferred_element_type=F32)
    acc_ref[g] = _lane_tile(alpha, pv.shape[1]) * acc_ref[g] + pv
    m_ref[g] = m_new


def _causal_mask(z):
    r = lax.broadcasted_iota(jnp.int32, z.shape, 0)
    c = lax.broadcasted_iota(jnp.int32, z.shape, 1)
    return jnp.where(c <= r, z, MASKED)


def _scores_nt(q, k):
    return lax.dot_general(q, k.astype(BF16), (((1,), (1,)), ((), ())), preferred_element_type=F32)


def _flash_init(m_ref, l_ref, acc_ref):
    m_ref[...] = jnp.full(m_ref.shape, MASKED, F32)
    l_ref[...] = jnp.zeros(l_ref.shape, F32)
    acc_ref[...] = jnp.zeros(acc_ref.shape, F32)


def _fox_flash_kernel(qi_tab, ki_tab, q_ref, k_ref, v_ref, kb_ref, o_ref, m_ref, l_ref, acc_ref, *, hp):
    s = pl.program_id(2)
    qi, ki = qi_tab[s], ki_tab[s]

    @pl.when(ki == 0)
    def _():
        _flash_init(m_ref, l_ref, acc_ref)

    def step(masked):
        for g in range(hp):
            cols = slice(g * HEAD_DIM, (g + 1) * HEAD_DIM)
            z = _scores_nt(q_ref[:, cols], k_ref[:, cols]) + kb_ref[g]
            if masked:
                z = _causal_mask(z)
            _softmax_update(z, v_ref[:, cols], m_ref, l_ref, acc_ref, g)

    @pl.when(ki < qi)
    def _():
        step(False)

    @pl.when(ki == qi)
    def _():
        step(True)
        for g in range(hp):
            o_ref[:, g * HEAD_DIM:(g + 1) * HEAD_DIM] = (acc_ref[g] / l_ref[g]).astype(o_ref.dtype)


def _diff_flash_kernel(qi_tab, ki_tab, q_ref, k_ref, v_ref, slope_ref, lam_ref, gain_ref, o_ref,
                       m_ref, l_ref, acc_ref, *, out_scale, hp):
    s = pl.program_id(2)
    qi, ki = qi_tab[s], ki_tab[s]
    tq, tk = q_ref.shape[0], k_ref.shape[0]
    dv = acc_ref.shape[2]

    @pl.when(ki == 0)
    def _():
        _flash_init(m_ref, l_ref, acc_ref)

    def step(masked):
        kpos = (lax.broadcasted_iota(jnp.int32, (1, tk), 1) + (ki * tk - qi * tq)).astype(F32)
        for hh in range(hp):
            kb = (slope_ref[pl.program_id(1) * hp + hh] * LOG2E) * kpos
            v = v_ref[:, hh * dv:(hh + 1) * dv]
            for g in range(2):
                cols = slice((2 * hh + g) * HEAD_DIM, (2 * hh + g + 1) * HEAD_DIM)
                z = _scores_nt(q_ref[:, cols], k_ref[:, cols]) + kb
                if masked:
                    z = _causal_mask(z)
                _softmax_update(z, v, m_ref, l_ref, acc_ref, 2 * hh + g)

    @pl.when(ki < qi)
    def _():
        step(False)

    @pl.when(ki == qi)
    def _():
        step(True)
        for hh in range(hp):
            a, b = 2 * hh, 2 * hh + 1
            o = acc_ref[a] / _lane_tile(l_ref[a], dv) - lam_ref[0] * (acc_ref[b] / _lane_tile(l_ref[b], dv))
            o_ref[:, hh * dv:(hh + 1) * dv] = (_rms_normed(o, gain_ref[...]) * out_scale).astype(o_ref.dtype)


def _pair_tables(nq):
    qi = [q for q in range(nq) for _ in range(q + 1)]
    ki = [k for q in range(nq) for k in range(q + 1)]
    return jnp.asarray(qi, jnp.int32), jnp.asarray(ki, jnp.int32)


def fox_flash(q, k, v, kb, n_heads, tq=1024, hp=2):
    b, t, _ = q.shape
    tq = _row_tile(t, tq)
    w = hp * HEAD_DIM
    qi_tab, ki_tab = _pair_tables(t // tq)
    blk = lambda tab: pl.BlockSpec((None, tq, w), lambda bi, h, s, qt, kt, tab=tab: (bi, (qt, kt)[tab][s], h))
    grid_spec = pltpu.PrefetchScalarGridSpec(
        num_scalar_prefetch=2,
        grid=(b, n_heads // hp, qi_tab.shape[0]),
        in_specs=[blk(0), blk(1), blk(1),
                  pl.BlockSpec((None, hp, 1, tq), lambda bi, h, s, qt, kt: (bi, h, 0, kt[s]))],
        out_specs=blk(0),
        scratch_shapes=[pltpu.VMEM((hp, tq, LANES), F32), pltpu.VMEM((hp, tq, LANES), F32),
                        pltpu.VMEM((hp, tq, HEAD_DIM), F32)],
    )
    return pl.pallas_call(
        functools.partial(_fox_flash_kernel, hp=hp),
        grid_spec=grid_spec,
        out_shape=jax.ShapeDtypeStruct(q.shape, BF16),
        compiler_params=_cparams("parallel", "parallel", "arbitrary"),
        name="fox_flash",
    )(qi_tab, ki_tab, q, k, v, kb.reshape(b, n_heads, 1, t))


def diff_flash(q, k, v, slopes, lam, subln_gain, out_scale, n_heads, tq=1024, hp=1):
    b, t, _ = q.shape
    tq = _row_tile(t, tq)
    dv = 2 * HEAD_DIM
    qi_tab, ki_tab = _pair_tables(t // tq)
    blk = lambda tab: pl.BlockSpec((None, tq, hp * dv), lambda bi, h, s, qt, kt, tab=tab: (bi, (qt, kt)[tab][s], h))
    smem = pl.BlockSpec(memory_space=pltpu.SMEM)
    grid_spec = pltpu.PrefetchScalarGridSpec(
        num_scalar_prefetch=2,
        grid=(b, n_heads // hp, qi_tab.shape[0]),
        in_specs=[blk(0), blk(1), blk(1), smem, smem,
                  pl.BlockSpec((1, dv), lambda bi, h, s, qt, kt: (0, 0))],
        out_specs=blk(0),
        scratch_shapes=[pltpu.VMEM((2 * hp, tq, LANES), F32), pltpu.VMEM((2 * hp, tq, LANES), F32),
                        pltpu.VMEM((2 * hp, tq, dv), F32)],
    )
    return pl.pallas_call(
        functools.partial(_diff_flash_kernel, out_scale=out_scale, hp=hp),
        grid_spec=grid_spec,
        out_shape=jax.ShapeDtypeStruct(q.shape, BF16),
        compiler_params=_cparams("parallel", "parallel", "arbitrary"),
        name="diff_flash",
    )(qi_tab, ki_tab, q, k, v, slopes, lam.reshape(1), subln_gain.reshape(1, dv))


def _lambda_kernel(p_ref, o_ref, *, lam_init):
    p = p_ref[...]
    s1 = jnp.sum(p[0:1] * p[1:2], axis=-1, keepdims=True)
    s2 = jnp.sum(p[2:3] * p[3:4], axis=-1, keepdims=True)
    o_ref[...] = jnp.broadcast_to(jnp.exp(s1) - jnp.exp(s2) + lam_init, o_ref.shape)


def diff_lambda(lq1, lk1, lq2, lk2, lam_init):
    p = jnp.stack([lq1, lk1, lq2, lk2]).astype(F32)
    out = pl.pallas_call(
        functools.partial(_lambda_kernel, lam_init=lam_init),
        out_shape=jax.ShapeDtypeStruct((1, LANES), F32),
        name="diff_lambda",
    )(p)
    return out[0, 0]


def _page_suffix_kernel(lf_ref, suf_ref, tot_ref):
    lf = lf_ref[...]
    r = lax.broadcasted_iota(jnp.int32, (PAGE, PAGE), 0)
    c = lax.broadcasted_iota(jnp.int32, (PAGE, PAGE), 1)
    after = (r > c).astype(F32)
    suf_ref[...] = jnp.dot(lf, after, precision=HIGHEST, preferred_element_type=F32)
    tot_ref[...] = jnp.broadcast_to(jnp.sum(lf, axis=-1, keepdims=True), lf.shape)


def page_suffix_sums(lf_rows, tr=1024):
    rows = lf_rows.shape[0]
    tr = _row_tile(rows, tr)
    spec = pl.BlockSpec((tr, PAGE), lambda i: (i, 0))
    return pl.pallas_call(
        _page_suffix_kernel,
        grid=(rows // tr,),
        in_specs=[spec],
        out_specs=[spec, spec],
        out_shape=[jax.ShapeDtypeStruct(lf_rows.shape, F32)] * 2,
        compiler_params=_cparams("parallel"),
        name="page_suffix",
    )(lf_rows)


def _head_mismatch_bias(rows, lanes, n_heads):
    r = lax.broadcasted_iota(jnp.int32, (rows, lanes), 0) % n_heads
    c = lax.broadcasted_iota(jnp.int32, (rows, lanes), 1) % n_heads
    return jnp.where(r == c, 0.0, MASKED).astype(F32)


def _decode_step(zs, v_tiles, m_ref, l_ref, acc_ref):
    m_prev = m_ref[...]
    m_new = m_prev
    for z in zs:
        m_new = jnp.maximum(m_new, jnp.max(z, axis=-1, keepdims=True))
    alpha = jnp.exp(m_prev - m_new)
    l_new = alpha * l_ref[...]
    acc = alpha * acc_ref[...]
    for z, v in zip(zs, v_tiles):
        p = jnp.exp(z - m_new)
        l_new = l_new + jnp.sum(p, axis=-1, keepdims=True)
        acc = acc + jnp.dot(p.astype(BF16), v.astype(BF16), preferred_element_type=F32)
    m_ref[...] = m_new
    l_ref[...] = l_new
    acc_ref[...] = acc


def _new_key_mask(z, n_heads, n_tok):
    r = lax.broadcasted_iota(jnp.int32, z.shape, 0)
    c = lax.broadcasted_iota(jnp.int32, z.shape, 1)
    ok = jnp.logical_and(c // n_heads <= (r // n_heads) % n_tok, c % n_heads == r % n_heads)
    return jnp.where(ok, z, MASKED)


def _decode_fox_kernel(pt_ref, q_ref, *refs, pps, n_heads, n_tok):
    k_refs, v_refs = refs[:pps], refs[pps:2 * pps]
    suf_refs, tot_refs = refs[2 * pps:3 * pps], refs[3 * pps:4 * pps]
    knew_ref, vnew_ref, lfnew_ref, o_ref, m_ref, l_ref, acc_ref, carry_ref = refs[4 * pps:]
    c = pl.program_id(1)

    @pl.when(c == 0)
    def _():
        _flash_init(m_ref, l_ref, acc_ref)
        carry_ref[...] = jnp.zeros(carry_ref.shape, F32)

    q = q_ref[...]
    mism = _head_mismatch_bias(q.shape[0], PAGE * n_heads, n_heads)
    carry = carry_ref[...]
    zs = [None] * pps
    for j in reversed(range(pps)):
        zs[j] = _scores_nt(q, k_refs[j][...]) + (suf_refs[j][...] + carry + mism)
        carry = carry + tot_refs[j][...]
    carry_ref[...] = carry
    _decode_step(zs, [v[...] for v in v_refs], m_ref, l_ref, acc_ref)

    @pl.when(c == pl.num_programs(1) - 1)
    def _():
        lf = lfnew_ref[...]
        cum = lf
        for i in range(1, n_tok):
            cum = cum + pltpu.roll(lf, i * n_heads, axis=1)
        z = _scores_nt(q, knew_ref[...]) - cum
        _decode_step([_new_key_mask(z, n_heads, n_tok)], [vnew_ref[...]], m_ref, l_ref, acc_ref)
        o_ref[...] = acc_ref[...] / l_ref[...]


def _decode_diff_kernel(pt_ref, q_ref, *refs, pps, n_heads, n_tok, past_len, out_scale):
    k_refs, v_refs = refs[:pps], refs[pps:2 * pps]
    (knew_ref, vnew_ref, slope_ref, lam_ref, gain_ref, o_ref, m_ref, l_ref, acc_ref) = refs[2 * pps:]
    c = pl.program_id(1)
    n_chunks = pl.num_programs(1)
    half = n_tok * n_heads
    rows_per_page = PAGE * n_heads

    @pl.when(c == 0)
    def _():
        _flash_init(m_ref, l_ref, acc_ref)

    q = q_ref[...]
    slope = slope_ref[...]
    key = lax.broadcasted_iota(jnp.int32, (1, rows_per_page), 1) // n_heads
    mism = _head_mismatch_bias(2 * half, rows_per_page, n_heads)
    zs = []
    for j in range(pps):
        first = ((n_chunks - 1 - c) * pps + j) * PAGE
        bias = slope * (key + (first - past_len)).astype(F32) + mism
        z = [_scores_nt(q[g * half:(g + 1) * half], k_refs[j][pl.ds(g, rows_per_page, stride=2), :])
             for g in range(2)]
        zs.append(jnp.concatenate(z, axis=0) + bias)
    _decode_step(zs, [v[...] for v in v_refs], m_ref, l_ref, acc_ref)

    @pl.when(c == n_chunks - 1)
    def _():
        z = [_scores_nt(q[g * half:(g + 1) * half], knew_ref[g]) for g in range(2)]
        lane = lax.broadcasted_iota(jnp.int32, (1, PAGE), 1)
        z = jnp.concatenate(z, axis=0) + slope[:, :PAGE] * (lane // n_heads).astype(F32)
        _decode_step([_new_key_mask(z, n_heads, n_tok)], [vnew_ref[...]], m_ref, l_ref, acc_ref)
        o = acc_ref[...] / l_ref[...]
        o = o[:half] - lam_ref[0] * o[half:]
        o_ref[...] = _rms_normed(o, gain_ref[...]) * out_scale


def _paged_specs(block, layer, n_pages, pps):
    def spec(j):
        def imap(bi, c, pt):
            return (layer, pt[bi, n_pages - (c + 1) * pps + j], 0, 0)
        return pl.BlockSpec((None, None) + block, imap)
    return [spec(j) for j in range(pps)]


def decode_fox(q, k_pool, v_pool, suf_pool, tot_pool, layer, page_table, k_new, v_new, lf_new, n_heads, n_tok, pps=8):
    b, rows, _ = q.shape
    n_pages = page_table.shape[1]
    page_rows = PAGE * n_heads
    per_b = lambda shp: pl.BlockSpec((None,) + shp, lambda bi, c, pt: (bi, 0, 0))
    grid_spec = pltpu.PrefetchScalarGridSpec(
        num_scalar_prefetch=1,
        grid=(b, n_pages // pps),
        in_specs=([per_b((rows, HEAD_DIM))]
                  + _paged_specs((page_rows, HEAD_DIM), layer, n_pages, pps) * 2
                  + _paged_specs((1, page_rows), 0, n_pages, pps) * 2
                  + [per_b((PAGE, HEAD_DIM)), per_b((PAGE, HEAD_DIM)), per_b((1, PAGE))]),
        out_specs=per_b((rows, HEAD_DIM)),
        scratch_shapes=[pltpu.VMEM((rows, 1), F32), pltpu.VMEM((rows, 1), F32),
                        pltpu.VMEM((rows, HEAD_DIM), F32), pltpu.VMEM((1, page_rows), F32)],
    )
    return pl.pallas_call(
        functools.partial(_decode_fox_kernel, pps=pps, n_heads=n_heads, n_tok=n_tok),
        grid_spec=grid_spec,
        out_shape=jax.ShapeDtypeStruct((b, rows, HEAD_DIM), F32),
        compiler_params=_cparams("parallel", "arbitrary"),
        name="decode_fox",
    )(page_table, q, *([k_pool] * pps), *([v_pool] * pps), *([suf_pool] * pps), *([tot_pool] * pps),
      k_new, v_new, lf_new)


def decode_diff(q, k_pool, v_pool, layer, page_table, k_new, v_new, slope_lanes, lam, subln_gain, out_scale,
                n_heads, n_tok, pps=4):
    b, rows, _ = q.shape
    dv = 2 * HEAD_DIM
    n_pages = page_table.shape[1]
    page_rows = PAGE * n_heads
    per_b = lambda shp: pl.BlockSpec((None,) + shp, lambda bi, c, pt: (bi,) + (0,) * len(shp))
    const = lambda shp: pl.BlockSpec(shp, lambda bi, c, pt: (0, 0))
    grid_spec = pltpu.PrefetchScalarGridSpec(
        num_scalar_prefetch=1,
        grid=(b, n_pages // pps),
        in_specs=([per_b((rows, HEAD_DIM))]
                  + _paged_specs((2 * page_rows, HEAD_DIM), layer, n_pages, pps)
                  + _paged_specs((page_rows, dv), layer, n_pages, pps)
                  + [per_b((2, PAGE, HEAD_DIM)), per_b((PAGE, dv)), const((1, page_rows)),
                     pl.BlockSpec(memory_space=pltpu.SMEM), const((1, dv))]),
        out_specs=per_b((rows // 2, dv)),
        scratch_shapes=[pltpu.VMEM((rows, 1), F32), pltpu.VMEM((rows, 1), F32),
                        pltpu.VMEM((rows, dv), F32)],
    )
    return pl.pallas_call(
        functools.partial(_decode_diff_kernel, pps=pps, n_heads=n_heads, n_tok=n_tok,
                          past_len=n_pages * PAGE, out_scale=out_scale),
        grid_spec=grid_spec,
        out_shape=jax.ShapeDtypeStruct((b, rows // 2, dv), F32),
        compiler_params=_cparams("parallel", "arbitrary"),
        name="decode_diff",
    )(page_table, q, *([k_pool] * pps), *([v_pool] * pps), k_new, v_new,
      slope_lanes, lam.reshape(1), subln_gain.reshape(1, dv))


MLP_SPLIT = (1, 2, 1, 1)
FOX_HEADS_PER_STEP = (4, 4)
DIFF_HEADS_PER_STEP = (1, 2)
FOX_PAGES_PER_STEP = (16, 16)
DIFF_PAGES_PER_STEP = (8, 8)


def _pad_rows(x, rows):
    pad = [(0, 0)] * x.ndim
    pad[-2] = (0, rows - x.shape[-2])
    return jnp.pad(x, pad)


def kernel(x_prompt, x_sample, cache_fox_k, cache_fox_v, cache_fox_logf, state_conv, cache_diff_k, cache_diff_v,
           page_table, attn_norm_gain, mlp_norm_gain, w_in_even, w_out_even, fox_q_gain, fox_k_gain, fox_f_bias,
           conv_w, w_in_odd, w_out_odd, diff_q_gain, diff_k_gain, diff_lq1, diff_lk1, diff_lq2, diff_lk2,
           diff_subln_gain, w_up, w_down):
    bp, tp, d = x_prompt.shape
    bs, ts, _ = x_sample.shape
    depth = attn_norm_gain.shape[0]
    n_fox_layers, n_diff_layers = w_in_even.shape[0], w_in_odd.shape[0]
    n_fox = cache_fox_k.shape[3]
    fox_w = n_fox * HEAD_DIM
    conv_c = state_conv.shape[-1]
    n_diff = cache_diff_k.shape[3]
    qk_w = n_diff * 2 * HEAD_DIM
    dv = cache_diff_v.shape[-1]
    v_w = n_diff * dv
    n_phys = cache_fox_k.shape[1]
    diff_v_rows = tuple((c % 2) * n_diff + c // 2 for c in range(v_w // HEAD_DIM))

    fox_k_pool = cache_fox_k.reshape(-1, n_phys, PAGE * n_fox, HEAD_DIM)
    fox_v_pool = cache_fox_v.reshape(-1, n_phys, PAGE * n_fox, HEAD_DIM)
    diff_k_pool = cache_diff_k.reshape(-1, n_phys, PAGE * n_diff * 2, HEAD_DIM)
    diff_v_pool = cache_diff_v.reshape(-1, n_phys, PAGE * n_diff, dv)
    slopes = jnp.asarray([2.0 ** (-8.0 * (h + 1) / n_diff) for h in range(n_diff)], F32)
    slope_lanes = jnp.tile(slopes, PAGE).reshape(1, PAGE * n_diff)

    xp = x_prompt.reshape(bp * tp, d)
    xs = x_sample.reshape(bs * ts, d)
    mp, ms = bp * tp, bs * ts
    ts_pad = SUBLANES
    leaf = {k: None for k in ("fk_p", "fv_p", "dk_p", "dv_p", "fk_s", "fv_s", "dk_s", "dv_s")}
    small = {k: [] for k in ("fl_p", "cs_p", "fl_s", "cs_s")}

    wc3, wf3 = even_tail_weights(w_in_even, 3 * fox_w, n_fox)
    w_up_bf, w_down_bf = w_up.astype(BF16), w_down.astype(BF16)
    hp = rmsnorm_bf16(xp, attn_norm_gain[0])
    hs = rmsnorm_bf16(xs, attn_norm_gain[0])

    for i in range(depth):
        j = i // 2
        if i % 2 == 0:
            bias = jnp.pad(fox_f_bias[j], (0, LANES - n_fox))
            lf_rows = jnp.swapaxes(cache_fox_logf[j], 1, 2).reshape(n_phys * n_fox, PAGE)
            suf, tot = page_suffix_sums(lf_rows)
            to_lanes = lambda a: jnp.swapaxes(a.reshape(n_phys, n_fox, PAGE), 1, 2).reshape(1, n_phys, 1, PAGE * n_fox)
            suf_pool, tot_pool = to_lanes(suf), to_lanes(tot)
            results = []
            for h_act, x_res, sample in ((hp, xp, False), (hs, xs, True)):
                tag = "_s" if sample else "_p"
                q_scale = QK_SCALE if sample else QK_SCALE * LOG2E
                qn = proj_heads(h_act, w_in_even, j, 0, fox_w, gain=fox_q_gain[j], scale=q_scale, name="fox_q")
                kn, leaf["fk" + tag] = proj_heads(h_act, w_in_even, j, fox_w, fox_w, gain=fox_k_gain[j],
                                                  leaf=(n_fox_layers, n_fox), leaf_prev=leaf["fk" + tag], name="fox_k")
                vv, leaf["fv" + tag] = proj_heads(h_act, w_in_even, j, 2 * fox_w, fox_w,
                                                  leaf=(n_fox_layers, n_fox), leaf_prev=leaf["fv" + tag], name="fox_v")
                lf = proj_logsigmoid(h_act, wf3, j, bias)
                cproj = proj_f32(h_act, wc3, j, name="conv_proj")
                if not sample:
                    kb = fox_key_bias(lf.reshape(bp, tp, LANES), n_fox)
                    attn = fox_flash(qn.reshape(bp, tp, fox_w), kn.reshape(bp, tp, fox_w),
                                     vv.reshape(bp, tp, fox_w), kb, n_fox, hp=FOX_HEADS_PER_STEP[j]).reshape(mp, fox_w)
                    yc, cs = conv_gate(cproj.reshape(bp, tp, 3 * conv_c), conv_w[j],
                                       jnp.zeros((bp, CONV_TAPS - 1, conv_c), F32), tp)
                    yc = yc.reshape(mp, conv_c)
                    small["fl_p"].append(lf.reshape(bp, tp, LANES)[..., :n_fox])
                    small["cs_p"].append(cs)
                else:
                    lf_s = lf.reshape(bs, ts, LANES)[..., :n_fox]
                    lf_new = _pad_rows(lf_s.reshape(bs, ts * n_fox, 1), PAGE).reshape(bs, 1, PAGE)
                    new_rows = lambda lv: _pad_rows(lv[j].reshape(bs, ts * n_fox, HEAD_DIM), PAGE)
                    attn = decode_fox(qn.reshape(bs, ts * n_fox, HEAD_DIM), fox_k_pool, fox_v_pool, suf_pool, tot_pool,
                                      j, page_table, new_rows(leaf["fk_s"]), new_rows(leaf["fv_s"]), lf_new, n_fox, ts,
                                      pps=FOX_PAGES_PER_STEP[j])
                    attn = attn.reshape(ms, fox_w).astype(BF16)
                    cpad = _pad_rows(cproj.reshape(bs, ts, 3 * conv_c), ts_pad)
                    yc, cs = conv_gate(cpad, conv_w[j], state_conv[j], ts)
                    yc = yc[:, :ts].reshape(ms, conv_c)
                    small["fl_s"].append(lf_s)
                    small["cs_s"].append(cs)
                results.append(proj_residual([attn, yc], w_out_even, j, x_res, name="even_out"))
            xp, xs = results
        else:
            lam_init = 0.8 - 0.6 * math.exp(-0.3 * i)
            lam = diff_lambda(diff_lq1[j], diff_lk1[j], diff_lq2[j], diff_lk2[j], lam_init)
            results = []
            for h_act, x_res, sample in ((hp, xp, False), (hs, xs, True)):
                tag = "_s" if sample else "_p"
                q_scale = QK_SCALE if sample else QK_SCALE * LOG2E
                qn = proj_heads(h_act, w_in_odd, j, 0, qk_w, gain=diff_q_gain[j], scale=q_scale, name="diff_q")
                kn, leaf["dk" + tag] = proj_heads(h_act, w_in_odd, j, qk_w, qk_w, gain=diff_k_gain[j],
                                                  leaf=(n_diff_layers, qk_w // HEAD_DIM), leaf_prev=leaf["dk" + tag],
                                                  name="diff_k")
                vv, leaf["dv" + tag] = proj_heads(h_act, w_in_odd, j, 2 * qk_w, v_w,
                                                  leaf=(n_diff_layers, v_w // HEAD_DIM), leaf_prev=leaf["dv" + tag],
                                                  leaf_rows=diff_v_rows, tm=512, tn=v_w, name="diff_v")
                if not sample:
                    o = diff_flash(qn.reshape(bp, tp, qk_w), kn.reshape(bp, tp, qk_w), vv.reshape(bp, tp, v_w),
                                   slopes, lam, diff_subln_gain[j], 1.0 - lam_init, n_diff,
                                   hp=DIFF_HEADS_PER_STEP[j])
                    o = o.reshape(mp, v_w)
                else:
                    split = lambda a: jnp.moveaxis(a.reshape(bs, ts, n_diff, 2, HEAD_DIM), 3, 1)
                    q_rows = split(qn).reshape(bs, 2 * ts * n_diff, HEAD_DIM)
                    k_new = _pad_rows(split(leaf["dk_s"][j]).reshape(bs, 2, ts * n_diff, HEAD_DIM), PAGE)
                    v_new = jnp.swapaxes(leaf["dv_s"][j].reshape(bs, ts, 2, n_diff, HEAD_DIM), 2, 3)
                    v_new = _pad_rows(v_new.reshape(bs, ts * n_diff, dv), PAGE)
                    o = decode_diff(q_rows, diff_k_pool, diff_v_pool, j, page_table, k_new, v_new,
                                    slope_lanes, lam, diff_subln_gain[j], 1.0 - lam_init, n_diff, ts,
                                    pps=DIFF_PAGES_PER_STEP[j])
                    o = o.reshape(ms, v_w).astype(BF16)
                results.append(proj_residual([o], w_out_odd, j, x_res, name="odd_out"))
            xp, xs = results
        if i + 1 < depth:
            xp, hp = mlp_residual(xp, mlp_norm_gain[i], w_up_bf, w_down_bf, i, next_gain=attn_norm_gain[i + 1],
                                  n_split=MLP_SPLIT[i])
            xs, hs = mlp_residual(xs, mlp_norm_gain[i], w_up_bf, w_down_bf, i, next_gain=attn_norm_gain[i + 1])
        else:
            xp = mlp_residual(xp, mlp_norm_gain[i], w_up_bf, w_down_bf, i, n_split=MLP_SPLIT[i])
            xs = mlp_residual(xs, mlp_norm_gain[i], w_up_bf, w_down_bf, i)

    st = {k: jnp.stack(v) for k, v in small.items()}
    fox_leaf = lambda a, b, t: a.reshape(n_fox_layers, b, t, n_fox, HEAD_DIM)
    dk_leaf = lambda a, b, t: a.reshape(n_diff_layers, b, t, n_diff, 2, HEAD_DIM)
    dv_leaf = lambda a, b, t: jnp.swapaxes(a.reshape(n_diff_layers, b, t, 2, n_diff, HEAD_DIM), 3, 4).reshape(
        n_diff_layers, b, t, n_diff, dv)
    return (xp.reshape(bp, tp, d), xs.reshape(bs, ts, d),
            fox_leaf(leaf["fk_p"], bp, tp), fox_leaf(leaf["fv_p"], bp, tp), st["fl_p"], st["cs_p"],
            dk_leaf(leaf["dk_p"], bp, tp), dv_leaf(leaf["dv_p"], bp, tp),
            fox_leaf(leaf["fk_s"], bs, ts), fox_leaf(leaf["fv_s"], bs, ts), st["fl_s"], st["cs_s"],
            dk_leaf(leaf["dk_s"], bs, ts), dv_leaf(leaf["dv_s"], bs, ts))
```

```python
import functools
import math

import jax
import jax.numpy as jnp
from jax import lax
from jax.experimental import pallas as pl
from jax.experimental.pallas import tpu as pltpu

F32 = jnp.float32
BF16 = jnp.bfloat16

HEAD_DIM = 128
LANES = 128
SUBLANES = 8
PAGE = 128
CONV_TAPS = 3
NORM_EPS = 1e-6
MASKED = -1e30
QK_SCALE = HEAD_DIM ** -0.5
LOG2E = math.log2(math.e)
VMEM_LIMIT = 56 * 1024 * 1024
HIGHEST = lax.Precision.HIGHEST


def _cparams(*sem):
    return pltpu.CompilerParams(dimension_semantics=sem, vmem_limit_bytes=VMEM_LIMIT)


def _row_tile(m, target):
    return m if m <= target else target


def _rms_normed(x, gain):
    ms = jnp.mean(x * x, axis=-1, keepdims=True)
    return x * lax.rsqrt(ms + NORM_EPS) * gain


def _rmsnorm_kernel(x_ref, g_ref, o_ref):
    o_ref[...] = _rms_normed(x_ref[...], g_ref[...]).astype(o_ref.dtype)


def rmsnorm_bf16(x, gain):
    m, d = x.shape
    tm = _row_tile(m, 512)
    return pl.pallas_call(
        _rmsnorm_kernel,
        grid=(m // tm,),
        in_specs=[pl.BlockSpec((tm, d), lambda i: (i, 0)),
                  pl.BlockSpec((1, d), lambda i: (0, 0))],
        out_specs=pl.BlockSpec((tm, d), lambda i: (i, 0)),
        out_shape=jax.ShapeDtypeStruct((m, d), BF16),
        compiler_params=_cparams("parallel"),
        name="rmsnorm",
    )(x, gain.reshape(1, d))


def _cached_bf16(w_ref, wb_ref):
    @pl.when(pl.program_id(1) == 0)
    def _():
        wb_ref[...] = w_ref[...].astype(BF16)


def _proj_headnorm_kernel(*refs, scale, leaf, alias):
    h_ref, w_ref, g_ref = refs[:3]
    outs = refs[3 + alias:]
    o_ref, wb_ref = outs[0], outs[-1]
    _cached_bf16(w_ref, wb_ref)
    acc = jnp.dot(h_ref[...], wb_ref[...], preferred_element_type=F32)
    g = g_ref[...]
    for hh in range(acc.shape[1] // HEAD_DIM):
        cols = slice(hh * HEAD_DIM, (hh + 1) * HEAD_DIM)
        blk = acc[:, cols]
        y = blk * lax.rsqrt(jnp.mean(blk * blk, axis=-1, keepdims=True) + NORM_EPS) * g
        if leaf:
            outs[1][:, hh, :] = y
        o_ref[:, cols] = (y * scale).astype(o_ref.dtype)


def _proj_plain_kernel(*refs, leaf_rows, alias):
    h_ref, w_ref = refs[:2]
    outs = refs[2 + alias:]
    o_ref, wb_ref = outs[0], outs[-1]
    _cached_bf16(w_ref, wb_ref)
    acc = jnp.dot(h_ref[...], wb_ref[...], preferred_element_type=F32)
    o_ref[...] = acc.astype(o_ref.dtype)
    if leaf_rows is not None:
        for cc, row in enumerate(leaf_rows):
            outs[1][:, row, :] = acc[:, cc * HEAD_DIM:(cc + 1) * HEAD_DIM]


def _proj_logsigmoid_kernel(h_ref, w_ref, b_ref, o_ref, wb_ref):
    _cached_bf16(w_ref, wb_ref)
    z = jnp.dot(h_ref[...], wb_ref[...], preferred_element_type=F32) + b_ref[...]
    o_ref[...] = jnp.minimum(z, 0.0) - jnp.log1p(jnp.exp(-jnp.abs(z)))


def _proj_residual_kernel(*refs, n_lhs):
    a_refs, w_refs = refs[:n_lhs], refs[n_lhs:2 * n_lhs]
    r_ref, o_ref = refs[2 * n_lhs], refs[2 * n_lhs + 1]
    wb_refs = refs[2 * n_lhs + 2:]
    acc = r_ref[...]
    for a_ref, w_ref, wb_ref in zip(a_refs, w_refs, wb_refs):
        _cached_bf16(w_ref, wb_ref)
        acc = acc + jnp.dot(a_ref[...], wb_ref[...], preferred_element_type=F32)
    o_ref[...] = acc


def _weight_spec(k_rows, tn, layer, row_blk, col_blk0):
    return pl.BlockSpec((None, k_rows, tn), lambda j, i: (layer, row_blk, col_blk0 + j),
                        pipeline_mode=pl.Buffered(1))


def _lhs_spec(tm, k):
    return pl.BlockSpec((tm, k), lambda j, i: (i, 0))


def _leaf_args(prev, n_layers, m, groups, layer, tm, chunks):
    spec = pl.BlockSpec((None, tm, chunks, LANES), lambda j, i: (layer, i, j, 0))
    shape = jax.ShapeDtypeStruct((n_layers, m, groups, LANES), F32)
    return spec, shape


def proj_heads(h, w3, layer, col0, n_cols, gain=None, scale=1.0, leaf=None, leaf_prev=None, leaf_rows=None,
               tm=1024, tn=1024, name="proj"):
    m, k = h.shape
    tm, tn = _row_tile(m, tm), _row_tile(n_cols, tn)
    chunks = tn // HEAD_DIM
    alias = leaf_prev is not None
    in_specs = [_lhs_spec(tm, k), _weight_spec(k, tn, layer, 0, col0 // tn)]
    args = [h, w3]
    if gain is not None:
        in_specs.append(pl.BlockSpec((1, HEAD_DIM), lambda j, i: (0, 0)))
        args.append(gain.reshape(1, HEAD_DIM).astype(F32))
    out_specs = [pl.BlockSpec((tm, tn), lambda j, i: (i, j))]
    out_shape = [jax.ShapeDtypeStruct((m, n_cols), BF16)]
    aliases = {}
    if leaf is not None:
        n_layers, groups = leaf
        rows_per_tile = chunks if leaf_rows is None else groups
        spec, shape = _leaf_args(leaf_prev, n_layers, m, groups, layer, tm, rows_per_tile)
        out_specs.append(spec)
        out_shape.append(shape)
        if alias:
            in_specs.append(pl.BlockSpec(memory_space=pl.ANY))
            args.append(leaf_prev)
            aliases = {len(args) - 1: 1}
    if gain is not None:
        body = functools.partial(_proj_headnorm_kernel, scale=scale, leaf=leaf is not None, alias=int(alias))
    else:
        rows = None if leaf is None else (leaf_rows if leaf_rows is not None else tuple(range(chunks)))
        body = functools.partial(_proj_plain_kernel, leaf_rows=rows, alias=int(alias))
    outs = pl.pallas_call(
        body,
        grid=(n_cols // tn, m // tm),
        in_specs=in_specs,
        out_specs=out_specs,
        out_shape=out_shape,
        scratch_shapes=[pltpu.VMEM((k, tn), BF16)],
        input_output_aliases=aliases,
        compiler_params=_cparams("parallel", "arbitrary"),
        name=name,
    )(*args)
    return outs if leaf is not None else outs[0]


def proj_logsigmoid(h, w3, layer, bias, name="proj_logf"):
    m, k = h.shape
    n = w3.shape[2]
    tm = _row_tile(m, 1024)
    return pl.pallas_call(
        _proj_logsigmoid_kernel,
        grid=(1, m // tm),
        in_specs=[_lhs_spec(tm, k), _weight_spec(k, n, layer, 0, 0), pl.BlockSpec((1, n), lambda j, i: (0, 0))],
        out_specs=pl.BlockSpec((tm, n), lambda j, i: (i, 0)),
        out_shape=jax.ShapeDtypeStruct((m, n), F32),
        scratch_shapes=[pltpu.VMEM((k, n), BF16)],
        compiler_params=_cparams("parallel", "arbitrary"),
        name=name,
    )(h, w3, bias.reshape(1, n))


def proj_f32(h, w3, layer, tm=1024, tn=1024, name="proj_f32"):
    m, k = h.shape
    n = w3.shape[2]
    tm, tn = _row_tile(m, tm), _row_tile(n, tn)
    return pl.pallas_call(
        functools.partial(_proj_plain_kernel, leaf_rows=None, alias=0),
        grid=(n // tn, m // tm),
        in_specs=[_lhs_spec(tm, k), _weight_spec(k, tn, layer, 0, 0)],
        out_specs=pl.BlockSpec((tm, tn), lambda j, i: (i, j)),
        out_shape=jax.ShapeDtypeStruct((m, n), F32),
        scratch_shapes=[pltpu.VMEM((k, tn), BF16)],
        compiler_params=_cparams("parallel", "arbitrary"),
        name=name,
    )(h, w3)


def _even_tail_kernel(w_ref, wc_ref, wf_ref, *, gate0, n_gates):
    x = w_ref[...]
    wc_ref[...] = x[:, gate0 + n_gates:].astype(BF16)
    g = x[:, gate0:gate0 + LANES]
    lane = lax.broadcasted_iota(jnp.int32, g.shape, 1)
    wf_ref[...] = jnp.where(lane < n_gates, g, 0.0).astype(BF16)


def even_tail_weights(w_in_even, gate0, n_gates, tk=256):
    n_layers, k, n = w_in_even.shape
    n_conv = n - gate0 - n_gates
    return pl.pallas_call(
        functools.partial(_even_tail_kernel, gate0=gate0, n_gates=n_gates),
        grid=(n_layers, k // tk),
        in_specs=[pl.BlockSpec((None, tk, n), lambda l, i: (l, i, 0))],
        out_specs=[pl.BlockSpec((None, tk, n_conv), lambda l, i: (l, i, 0)),
                   pl.BlockSpec((None, tk, LANES), lambda l, i: (l, i, 0))],
        out_shape=[jax.ShapeDtypeStruct((n_layers, k, n_conv), BF16),
                   jax.ShapeDtypeStruct((n_layers, k, LANES), BF16)],
        compiler_params=_cparams("parallel", "parallel"),
        name="even_tail_weights",
    )(w_in_even)


def proj_residual(lhs_list, w3, layer, res, tm=1024, tn=1024, name="proj_out"):
    m, n = res.shape
    tm, tn = _row_tile(m, tm), _row_tile(n, tn)
    k = lhs_list[0].shape[1]
    in_specs = [_lhs_spec(tm, k) for _ in lhs_list]
    in_specs += [_weight_spec(k, tn, layer, r, 0) for r in range(len(lhs_list))]
    in_specs.append(pl.BlockSpec((tm, tn), lambda j, i: (i, j)))
    return pl.pallas_call(
        functools.partial(_proj_residual_kernel, n_lhs=len(lhs_list)),
        grid=(n // tn, m // tm),
        in_specs=in_specs,
        out_specs=pl.BlockSpec((tm, tn), lambda j, i: (i, j)),
        out_shape=jax.ShapeDtypeStruct((m, n), F32),
        scratch_shapes=[pltpu.VMEM((k, tn), BF16) for _ in lhs_list],
        compiler_params=_cparams("parallel", "arbitrary"),
        name=name,
    )(*lhs_list, *([w3] * len(lhs_list)), res)


def _mlp_kernel(*refs, with_next):
    x_ref, g_ref, wu_ref, wd_ref = refs[:4]
    o_ref, h_ref = refs[4 + with_next], refs[-1]
    f = pl.program_id(1)

    @pl.when(f == 0)
    def _():
        h_ref[...] = _rms_normed(x_ref[...], g_ref[...]).astype(BF16)

    a = jnp.dot(h_ref[...], wu_ref[...], preferred_element_type=F32)
    a = jnp.maximum(a, 0.0)
    part = jnp.dot((a * a).astype(BF16), wd_ref[...], preferred_element_type=F32)

    @pl.when(f == 0)
    def _():
        o_ref[...] = x_ref[...] + part

    @pl.when(f != 0)
    def _():
        o_ref[...] += part

    if with_next:
        @pl.when(f == pl.num_programs(1) - 1)
        def _():
            refs[6][...] = _rms_normed(o_ref[...], refs[4][...]).astype(BF16)


def mlp_residual(x, gain, w_up, w_down, layer, next_gain=None, tm=512, tf=1024):
    m, d = x.shape
    ff = w_up.shape[2]
    tm = _row_tile(m, tm)
    with_next = next_gain is not None
    row_spec = lambda **kw: pl.BlockSpec((tm, d), lambda i, f: (i, 0), **kw)
    vec_spec = pl.BlockSpec((1, d), lambda i, f: (0, 0))
    in_specs = [row_spec(), vec_spec,
                pl.BlockSpec((None, d, tf), lambda i, f: (layer, 0, f)),
                pl.BlockSpec((None, tf, d), lambda i, f: (layer, f, 0))]
    args = [x, gain.reshape(1, d), w_up, w_down]
    out_specs, out_shape = [row_spec()], [jax.ShapeDtypeStruct((m, d), F32)]
    if with_next:
        in_specs.append(vec_spec)
        args.append(next_gain.reshape(1, d))
        out_specs.append(row_spec())
        out_shape.append(jax.ShapeDtypeStruct((m, d), BF16))
    outs = pl.pallas_call(
        functools.partial(_mlp_kernel, with_next=int(with_next)),
        grid=(m // tm, ff // tf),
        in_specs=in_specs,
        out_specs=out_specs,
        out_shape=out_shape,
        scratch_shapes=[pltpu.VMEM((tm, d), BF16)],
        compiler_params=_cparams("parallel", "arbitrary"),
        name="mlp",
    )(*args)
    return outs if with_next else outs[0]


def _conv_kernel(cx_ref, gb_ref, gc_ref, w_ref, st_ref, y_ref, so_ref, carry_ref, *, last):
    t = pl.program_id(1)

    @pl.when(t == 0)
    def _():
        carry_ref[...] = st_ref[...]

    u = gc_ref[...] * cx_ref[...]
    rows = lax.broadcasted_iota(jnp.int32, u.shape, 0)
    prev1 = carry_ref[1:2, :]
    prev2 = carry_ref[0:1, :]
    u1 = jnp.where(rows == 0, prev1, pltpu.roll(u, 1, axis=0))
    u2 = pltpu.roll(u, 2, axis=0)
    u2 = jnp.where(rows == 0, prev2, jnp.where(rows == 1, prev1, u2))
    conv = w_ref[0:1, :] * u2 + w_ref[1:2, :] * u1 + w_ref[2:3, :] * u
    y_ref[...] = (gb_ref[...] * conv).astype(y_ref.dtype)
    tail = u[last - 2:last, :]
    carry_ref[...] = tail
    so_ref[...] = tail


def conv_gate(proj, conv_w, state, t_valid, tt_target=512):
    b, t, c3 = proj.shape
    c = c3 // 3
    tt = _row_tile(t, tt_target)
    assert t_valid == t or t == tt
    sec = lambda k: pl.BlockSpec((None, tt, c), lambda bi, ti, k=k: (bi, ti, k))
    return pl.pallas_call(
        functools.partial(_conv_kernel, last=t_valid - (t - tt)),
        grid=(b, t // tt),
        in_specs=[sec(0), sec(1), sec(2),
                  pl.BlockSpec((CONV_TAPS, c), lambda bi, ti: (0, 0)),
                  pl.BlockSpec((None, CONV_TAPS - 1, c), lambda bi, ti: (bi, 0, 0))],
        out_specs=[pl.BlockSpec((None, tt, c), lambda bi, ti: (bi, ti, 0)),
                   pl.BlockSpec((None, CONV_TAPS - 1, c), lambda bi, ti: (bi, 0, 0))],
        out_shape=[jax.ShapeDtypeStruct((b, t, c), BF16),
                   jax.ShapeDtypeStruct((b, CONV_TAPS - 1, c), F32)],
        scratch_shapes=[pltpu.VMEM((CONV_TAPS - 1, c), F32)],
        compiler_params=_cparams("parallel", "arbitrary"),
        name="conv_gate",
    )(proj, proj, proj, conv_w, state)


def _cumsum_kernel(lf_ref, kb_ref, *, chunk):
    t = lf_ref.shape[0]
    r = lax.broadcasted_iota(jnp.int32, (chunk, chunk), 0)
    c = lax.broadcasted_iota(jnp.int32, (chunk, chunk), 1)
    upper = (r <= c).astype(F32)
    carry = jnp.zeros((LANES, 1), F32)
    for i in range(t // chunk):
        lf_t = lf_ref[i * chunk:(i + 1) * chunk, :].T
        cs = jnp.dot(lf_t, upper, precision=HIGHEST, preferred_element_type=F32) + carry
        kb_ref[:, i * chunk:(i + 1) * chunk] = cs[:kb_ref.shape[0], :] * (-LOG2E)
        carry = cs[:, chunk - 1:chunk]


def fox_key_bias(logf_padded, n_heads, chunk=512):
    b, t, w = logf_padded.shape
    chunk = _row_tile(t, chunk)
    return pl.pallas_call(
        functools.partial(_cumsum_kernel, chunk=chunk),
        grid=(b,),
        in_specs=[pl.BlockSpec((None, t, w), lambda bi: (bi, 0, 0))],
        out_specs=pl.BlockSpec((None, n_heads, t), lambda bi: (bi, 0, 0)),
        out_shape=jax.ShapeDtypeStruct((b, n_heads, t), F32),
        compiler_params=_cparams("parallel"),
        name="fox_cumsum",
    )(logf_padded)


def _lane_tile(x, width):
    return x if width == LANES else jnp.concatenate([x] * (width // LANES), axis=1)


def _softmax_update(z, v, m_ref, l_ref, acc_ref, g):
    m_prev = m_ref[g]
    m_new = jnp.maximum(m_prev, jnp.max(z, axis=-1, keepdims=True))
    alpha = jnp.exp2(m_prev - m_new)
    p = jnp.exp2(z - _lane_tile(m_new, z.shape[1]))
    l_ref[g] = alpha * l_ref[g] + jnp.sum(p, axis=-1, keepdims=True)
    pv = jnp.dot(p.astype(BF16), v, preferred_element_type=F32)
    acc_ref[g] = _lane_tile(alpha, pv.shape[1]) * acc_ref[g] + pv
    m_ref[g] = m_new


def _causal_mask(z):
    r = lax.broadcasted_iota(jnp.int32, z.shape, 0)
    c = lax.broadcasted_iota(jnp.int32, z.shape, 1)
    return jnp.where(c <= r, z, MASKED)


def _scores_nt(q, k):
    return lax.dot_general(q, k.astype(BF16), (((1,), (1,)), ((), ())), preferred_element_type=F32)


def _flash_init(m_ref, l_ref, acc_ref):
    m_ref[...] = jnp.full(m_ref.shape, MASKED, F32)
    l_ref[...] = jnp.zeros(l_ref.shape, F32)
    acc_ref[...] = jnp.zeros(acc_ref.shape, F32)


def _fox_flash_kernel(qi_tab, ki_tab, q_ref, k_ref, v_ref, kb_ref, o_ref, m_ref, l_ref, acc_ref, *, hp):
    s = pl.program_id(2)
    qi, ki = qi_tab[s], ki_tab[s]

    @pl.when(ki == 0)
    def _():
        _flash_init(m_ref, l_ref, acc_ref)

    def step(masked):
        for g in range(hp):
            cols = slice(g * HEAD_DIM, (g + 1) * HEAD_DIM)
            z = _scores_nt(q_ref[:, cols], k_ref[:, cols]) + kb_ref[g]
            if masked:
                z = _causal_mask(z)
            _softmax_update(z, v_ref[:, cols], m_ref, l_ref, acc_ref, g)

    @pl.when(ki < qi)
    def _():
        step(False)

    @pl.when(ki == qi)
    def _():
        step(True)
        for g in range(hp):
            o_ref[:, g * HEAD_DIM:(g + 1) * HEAD_DIM] = (acc_ref[g] / l_ref[g]).astype(o_ref.dtype)


def _diff_flash_kernel(qi_tab, ki_tab, q_ref, k_ref, v_ref, slope_ref, lam_ref, gain_ref, o_ref,
                       m_ref, l_ref, acc_ref, *, out_scale, hp):
    s = pl.program_id(2)
    qi, ki = qi_tab[s], ki_tab[s]
    tq, tk = q_ref.shape[0], k_ref.shape[0]
    dv = acc_ref.shape[2]

    @pl.when(ki == 0)
    def _():
        _flash_init(m_ref, l_ref, acc_ref)

    def step(masked):
        kpos = (lax.broadcasted_iota(jnp.int32, (1, tk), 1) + (ki * tk - qi * tq)).astype(F32)
        for hh in range(hp):
            kb = (slope_ref[pl.program_id(1) * hp + hh] * LOG2E) * kpos
            v = v_ref[:, hh * dv:(hh + 1) * dv]
            for g in range(2):
                cols = slice((2 * hh + g) * HEAD_DIM, (2 * hh + g + 1) * HEAD_DIM)
                z = _scores_nt(q_ref[:, cols], k_ref[:, cols]) + kb
                if masked:
                    z = _causal_mask(z)
                _softmax_update(z, v, m_ref, l_ref, acc_ref, 2 * hh + g)

    @pl.when(ki < qi)
    def _():
        step(False)

    @pl.when(ki == qi)
    def _():
        step(True)
        for hh in range(hp):
            a, b = 2 * hh, 2 * hh + 1
            o = acc_ref[a] / _lane_tile(l_ref[a], dv) - lam_ref[0] * (acc_ref[b] / _lane_tile(l_ref[b], dv))
            o_ref[:, hh * dv:(hh + 1) * dv] = (_rms_normed(o, gain_ref[...]) * out_scale).astype(o_ref.dtype)


def _pair_tables(nq):
    qi = [q for q in range(nq) for _ in range(q + 1)]
    ki = [k for q in range(nq) for k in range(q + 1)]
    return jnp.asarray(qi, jnp.int32), jnp.asarray(ki, jnp.int32)


def fox_flash(q, k, v, kb, n_heads, tq=1024, hp=2):
    b, t, _ = q.shape
    tq = _row_tile(t, tq)
    w = hp * HEAD_DIM
    qi_tab, ki_tab = _pair_tables(t // tq)
    blk = lambda tab: pl.BlockSpec((None, tq, w), lambda bi, h, s, qt, kt, tab=tab: (bi, (qt, kt)[tab][s], h))
    grid_spec = pltpu.PrefetchScalarGridSpec(
        num_scalar_prefetch=2,
        grid=(b, n_heads // hp, qi_tab.shape[0]),
        in_specs=[blk(0), blk(1), blk(1),
                  pl.BlockSpec((None, hp, 1, tq), lambda bi, h, s, qt, kt: (bi, h, 0, kt[s]))],
        out_specs=blk(0),
        scratch_shapes=[pltpu.VMEM((hp, tq, LANES), F32), pltpu.VMEM((hp, tq, LANES), F32),
                        pltpu.VMEM((hp, tq, HEAD_DIM), F32)],
    )
    return pl.pallas_call(
        functools.partial(_fox_flash_kernel, hp=hp),
        grid_spec=grid_spec,
        out_shape=jax.ShapeDtypeStruct(q.shape, BF16),
        compiler_params=_cparams("parallel", "parallel", "arbitrary"),
        name="fox_flash",
    )(qi_tab, ki_tab, q, k, v, kb.reshape(b, n_heads, 1, t))


def diff_flash(q, k, v, slopes, lam, subln_gain, out_scale, n_heads, tq=1024, hp=1):
    b, t, _ = q.shape
    tq = _row_tile(t, tq)
    dv = 2 * HEAD_DIM
    qi_tab, ki_tab = _pair_tables(t // tq)
    blk = lambda tab: pl.BlockSpec((None, tq, hp * dv), lambda bi, h, s, qt, kt, tab=tab: (bi, (qt, kt)[tab][s], h))
    smem = pl.BlockSpec(memory_space=pltpu.SMEM)
    grid_spec = pltpu.PrefetchScalarGridSpec(
        num_scalar_prefetch=2,
        grid=(b, n_heads // hp, qi_tab.shape[0]),
        in_specs=[blk(0), blk(1), blk(1), smem, smem,
                  pl.BlockSpec((1, dv), lambda bi, h, s, qt, kt: (0, 0))],
        out_specs=blk(0),
        scratch_shapes=[pltpu.VMEM((2 * hp, tq, LANES), F32), pltpu.VMEM((2 * hp, tq, LANES), F32),
                        pltpu.VMEM((2 * hp, tq, dv), F32)],
    )
    return pl.pallas_call(
        functools.partial(_diff_flash_kernel, out_scale=out_scale, hp=hp),
        grid_spec=grid_spec,
        out_shape=jax.ShapeDtypeStruct(q.shape, BF16),
        compiler_params=_cparams("parallel", "parallel", "arbitrary"),
        name="diff_flash",
    )(qi_tab, ki_tab, q, k, v, slopes, lam.reshape(1), subln_gain.reshape(1, dv))


def _lambda_kernel(p_ref, o_ref, *, lam_init):
    p = p_ref[...]
    s1 = jnp.sum(p[0:1] * p[1:2], axis=-1, keepdims=True)
    s2 = jnp.sum(p[2:3] * p[3:4], axis=-1, keepdims=True)
    o_ref[...] = jnp.broadcast_to(jnp.exp(s1) - jnp.exp(s2) + lam_init, o_ref.shape)


def diff_lambda(lq1, lk1, lq2, lk2, lam_init):
    p = jnp.stack([lq1, lk1, lq2, lk2]).astype(F32)
    out = pl.pallas_call(
        functools.partial(_lambda_kernel, lam_init=lam_init),
        out_shape=jax.ShapeDtypeStruct((1, LANES), F32),
        name="diff_lambda",
    )(p)
    return out[0, 0]


def _page_suffix_kernel(lf_ref, suf_ref, tot_ref):
    lf = lf_ref[...]
    r = lax.broadcasted_iota(jnp.int32, (PAGE, PAGE), 0)
    c = lax.broadcasted_iota(jnp.int32, (PAGE, PAGE), 1)
    after = (r > c).astype(F32)
    suf_ref[...] = jnp.dot(lf, after, precision=HIGHEST, preferred_element_type=F32)
    tot_ref[...] = jnp.broadcast_to(jnp.sum(lf, axis=-1, keepdims=True), lf.shape)


def page_suffix_sums(lf_rows, tr=1024):
    rows = lf_rows.shape[0]
    tr = _row_tile(rows, tr)
    spec = pl.BlockSpec((tr, PAGE), lambda i: (i, 0))
    return pl.pallas_call(
        _page_suffix_kernel,
        grid=(rows // tr,),
        in_specs=[spec],
        out_specs=[spec, spec],
        out_shape=[jax.ShapeDtypeStruct(lf_rows.shape, F32)] * 2,
        compiler_params=_cparams("parallel"),
        name="page_suffix",
    )(lf_rows)


def _head_mismatch_bias(rows, lanes, n_heads):
    r = lax.broadcasted_iota(jnp.int32, (rows, lanes), 0) % n_heads
    c = lax.broadcasted_iota(jnp.int32, (rows, lanes), 1) % n_heads
    return jnp.where(r == c, 0.0, MASKED).astype(F32)


def _decode_step(zs, v_tiles, m_ref, l_ref, acc_ref):
    m_prev = m_ref[...]
    m_new = m_prev
    for z in zs:
        m_new = jnp.maximum(m_new, jnp.max(z, axis=-1, keepdims=True))
    alpha = jnp.exp(m_prev - m_new)
    l_new = alpha * l_ref[...]
    acc = alpha * acc_ref[...]
    for z, v in zip(zs, v_tiles):
        p = jnp.exp(z - m_new)
        l_new = l_new + jnp.sum(p, axis=-1, keepdims=True)
        acc = acc + jnp.dot(p.astype(BF16), v.astype(BF16), preferred_element_type=F32)
    m_ref[...] = m_new
    l_ref[...] = l_new
    acc_ref[...] = acc


def _new_key_mask(z, n_heads, n_tok):
    r = lax.broadcasted_iota(jnp.int32, z.shape, 0)
    c = lax.broadcasted_iota(jnp.int32, z.shape, 1)
    ok = jnp.logical_and(c // n_heads <= (r // n_heads) % n_tok, c % n_heads == r % n_heads)
    return jnp.where(ok, z, MASKED)


def _decode_fox_kernel(pt_ref, q_ref, *refs, pps, n_heads, n_tok):
    k_refs, v_refs = refs[:pps], refs[pps:2 * pps]
    suf_refs, tot_refs = refs[2 * pps:3 * pps], refs[3 * pps:4 * pps]
    knew_ref, vnew_ref, lfnew_ref, o_ref, m_ref, l_ref, acc_ref, carry_ref = refs[4 * pps:]
    c = pl.program_id(1)

    @pl.when(c == 0)
    def _():
        _flash_init(m_ref, l_ref, acc_ref)
        carry_ref[...] = jnp.zeros(carry_ref.shape, F32)

    q = q_ref[...]
    mism = _head_mismatch_bias(q.shape[0], PAGE * n_heads, n_heads)
    carry = carry_ref[...]
    zs = [None] * pps
    for j in reversed(range(pps)):
        zs[j] = _scores_nt(q, k_refs[j][...]) + (suf_refs[j][...] + carry + mism)
        carry = carry + tot_refs[j][...]
    carry_ref[...] = carry
    _decode_step(zs, [v[...] for v in v_refs], m_ref, l_ref, acc_ref)

    @pl.when(c == pl.num_programs(1) - 1)
    def _():
        lf = lfnew_ref[...]
        cum = lf
        for i in range(1, n_tok):
            cum = cum + pltpu.roll(lf, i * n_heads, axis=1)
        z = _scores_nt(q, knew_ref[...]) - cum
        _decode_step([_new_key_mask(z, n_heads, n_tok)], [vnew_ref[...]], m_ref, l_ref, acc_ref)
        o_ref[...] = acc_ref[...] / l_ref[...]


def _decode_diff_kernel(pt_ref, q_ref, *refs, pps, n_heads, n_tok, past_len, out_scale):
    k_refs, v_refs = refs[:pps], refs[pps:2 * pps]
    (knew_ref, vnew_ref, slope_ref, lam_ref, gain_ref, o_ref, m_ref, l_ref, acc_ref) = refs[2 * pps:]
    c = pl.program_id(1)
    n_chunks = pl.num_programs(1)
    half = n_tok * n_heads
    rows_per_page = PAGE * n_heads

    @pl.when(c == 0)
    def _():
        _flash_init(m_ref, l_ref, acc_ref)

    q = q_ref[...]
    slope = slope_ref[...]
    key = lax.broadcasted_iota(jnp.int32, (1, rows_per_page), 1) // n_heads
    mism = _head_mismatch_bias(2 * half, rows_per_page, n_heads)
    zs = []
    for j in range(pps):
        first = ((n_chunks - 1 - c) * pps + j) * PAGE
        bias = slope * (key + (first - past_len)).astype(F32) + mism
        z = [_scores_nt(q[g * half:(g + 1) * half], k_refs[j][pl.ds(g, rows_per_page, stride=2), :])
             for g in range(2)]
        zs.append(jnp.concatenate(z, axis=0) + bias)
    _decode_step(zs, [v[...] for v in v_refs], m_ref, l_ref, acc_ref)

    @pl.when(c == n_chunks - 1)
    def _():
        z = [_scores_nt(q[g * half:(g + 1) * half], knew_ref[g]) for g in range(2)]
        lane = lax.broadcasted_iota(jnp.int32, (1, PAGE), 1)
        z = jnp.concatenate(z, axis=0) + slope[:, :PAGE] * (lane // n_heads).astype(F32)
        _decode_step([_new_key_mask(z, n_heads, n_tok)], [vnew_ref[...]], m_ref, l_ref, acc_ref)
        o = acc_ref[...] / l_ref[...]
        o = o[:half] - lam_ref[0] * o[half:]
        o_ref[...] = _rms_normed(o, gain_ref[...]) * out_scale


def _paged_specs(block, layer, n_pages, pps):
    def spec(j):
        def imap(bi, c, pt):
            return (layer, pt[bi, n_pages - (c + 1) * pps + j], 0, 0)
        return pl.BlockSpec((None, None) + block, imap)
    return [spec(j) for j in range(pps)]


def decode_fox(q, k_pool, v_pool, suf_pool, tot_pool, layer, page_table, k_new, v_new, lf_new, n_heads, n_tok, pps=8):
    b, rows, _ = q.shape
    n_pages = page_table.shape[1]
    page_rows = PAGE * n_heads
    per_b = lambda shp: pl.BlockSpec((None,) + shp, lambda bi, c, pt: (bi, 0, 0))
    grid_spec = pltpu.PrefetchScalarGridSpec(
        num_scalar_prefetch=1,
        grid=(b, n_pages // pps),
        in_specs=([per_b((rows, HEAD_DIM))]
                  + _paged_specs((page_rows, HEAD_DIM), layer, n_pages, pps) * 2
                  + _paged_specs((1, page_rows), 0, n_pages, pps) * 2
                  + [per_b((PAGE, HEAD_DIM)), per_b((PAGE, HEAD_DIM)), per_b((1, PAGE))]),
        out_specs=per_b((rows, HEAD_DIM)),
        scratch_shapes=[pltpu.VMEM((rows, 1), F32), pltpu.VMEM((rows, 1), F32),
                        pltpu.VMEM((rows, HEAD_DIM), F32), pltpu.VMEM((1, page_rows), F32)],
    )
    return pl.pallas_call(
        functools.partial(_decode_fox_kernel, pps=pps, n_heads=n_heads, n_tok=n_tok),
        grid_spec=grid_spec,
        out_shape=jax.ShapeDtypeStruct((b, rows, HEAD_DIM), F32),
        compiler_params=_cparams("parallel", "arbitrary"),
        name="decode_fox",
    )(page_table, q, *([k_pool] * pps), *([v_pool] * pps), *([suf_pool] * pps), *([tot_pool] * pps),
      k_new, v_new, lf_new)


def decode_diff(q, k_pool, v_pool, layer, page_table, k_new, v_new, slope_lanes, lam, subln_gain, out_scale,
                n_heads, n_tok, pps=4):
    b, rows, _ = q.shape
    dv = 2 * HEAD_DIM
    n_pages = page_table.shape[1]
    page_rows = PAGE * n_heads
    per_b = lambda shp: pl.BlockSpec((None,) + shp, lambda bi, c, pt: (bi,) + (0,) * len(shp))
    const = lambda shp: pl.BlockSpec(shp, lambda bi, c, pt: (0, 0))
    grid_spec = pltpu.PrefetchScalarGridSpec(
        num_scalar_prefetch=1,
        grid=(b, n_pages // pps),
        in_specs=([per_b((rows, HEAD_DIM))]
                  + _paged_specs((2 * page_rows, HEAD_DIM), layer, n_pages, pps)
                  + _paged_specs((page_rows, dv), layer, n_pages, pps)
                  + [per_b((2, PAGE, HEAD_DIM)), per_b((PAGE, dv)), const((1, page_rows)),
                     pl.BlockSpec(memory_space=pltpu.SMEM), const((1, dv))]),
        out_specs=per_b((rows // 2, dv)),
        scratch_shapes=[pltpu.VMEM((rows, 1), F32), pltpu.VMEM((rows, 1), F32),
                        pltpu.VMEM((rows, dv), F32)],
    )
    return pl.pallas_call(
        functools.partial(_decode_diff_kernel, pps=pps, n_heads=n_heads, n_tok=n_tok,
                          past_len=n_pages * PAGE, out_scale=out_scale),
        grid_spec=grid_spec,
        out_shape=jax.ShapeDtypeStruct((b, rows // 2, dv), F32),
        compiler_params=_cparams("parallel", "arbitrary"),
        name="decode_diff",
    )(page_table, q, *([k_pool] * pps), *([v_pool] * pps), k_new, v_new,
      slope_lanes, lam.reshape(1), subln_gain.reshape(1, dv))


FOX_HEADS_PER_STEP = 4
DIFF_HEADS_PER_STEP = 2
FOX_PAGES_PER_STEP = 16
DIFF_PAGES_PER_STEP = 8


def _pad_rows(x, rows):
    pad = [(0, 0)] * x.ndim
    pad[-2] = (0, rows - x.shape[-2])
    return jnp.pad(x, pad)


def kernel(x_prompt, x_sample, cache_fox_k, cache_fox_v, cache_fox_logf, state_conv, cache_diff_k, cache_diff_v,
           page_table, attn_norm_gain, mlp_norm_gain, w_in_even, w_out_even, fox_q_gain, fox_k_gain, fox_f_bias,
           conv_w, w_in_odd, w_out_odd, diff_q_gain, diff_k_gain, diff_lq1, diff_lk1, diff_lq2, diff_lk2,
           diff_subln_gain, w_up, w_down):
    bp, tp, d = x_prompt.shape
    bs, ts, _ = x_sample.shape
    depth = attn_norm_gain.shape[0]
    n_fox_layers, n_diff_layers = w_in_even.shape[0], w_in_odd.shape[0]
    n_fox = cache_fox_k.shape[3]
    fox_w = n_fox * HEAD_DIM
    conv_c = state_conv.shape[-1]
    n_diff = cache_diff_k.shape[3]
    qk_w = n_diff * 2 * HEAD_DIM
    dv = cache_diff_v.shape[-1]
    v_w = n_diff * dv
    n_phys = cache_fox_k.shape[1]
    diff_v_rows = tuple((c % 2) * n_diff + c // 2 for c in range(v_w // HEAD_DIM))

    fox_k_pool = cache_fox_k.reshape(-1, n_phys, PAGE * n_fox, HEAD_DIM)
    fox_v_pool = cache_fox_v.reshape(-1, n_phys, PAGE * n_fox, HEAD_DIM)
    diff_k_pool = cache_diff_k.reshape(-1, n_phys, PAGE * n_diff * 2, HEAD_DIM)
    diff_v_pool = cache_diff_v.reshape(-1, n_phys, PAGE * n_diff, dv)
    slopes = jnp.asarray([2.0 ** (-8.0 * (h + 1) / n_diff) for h in range(n_diff)], F32)
    slope_lanes = jnp.tile(slopes, PAGE).reshape(1, PAGE * n_diff)

    xp = x_prompt.reshape(bp * tp, d)
    xs = x_sample.reshape(bs * ts, d)
    mp, ms = bp * tp, bs * ts
    ts_pad = SUBLANES
    leaf = {k: None for k in ("fk_p", "fv_p", "dk_p", "dv_p", "fk_s", "fv_s", "dk_s", "dv_s")}
    small = {k: [] for k in ("fl_p", "cs_p", "fl_s", "cs_s")}

    wc3, wf3 = even_tail_weights(w_in_even, 3 * fox_w, n_fox)
    w_up_bf, w_down_bf = w_up.astype(BF16), w_down.astype(BF16)
    hp = rmsnorm_bf16(xp, attn_norm_gain[0])
    hs = rmsnorm_bf16(xs, attn_norm_gain[0])

    for i in range(depth):
        j = i // 2
        if i % 2 == 0:
            bias = jnp.pad(fox_f_bias[j], (0, LANES - n_fox))
            lf_rows = jnp.swapaxes(cache_fox_logf[j], 1, 2).reshape(n_phys * n_fox, PAGE)
            suf, tot = page_suffix_sums(lf_rows)
            to_lanes = lambda a: jnp.swapaxes(a.reshape(n_phys, n_fox, PAGE), 1, 2).reshape(1, n_phys, 1, PAGE * n_fox)
            suf_pool, tot_pool = to_lanes(suf), to_lanes(tot)
            results = []
            for h_act, x_res, sample in ((hp, xp, False), (hs, xs, True)):
                tag = "_s" if sample else "_p"
                q_scale = QK_SCALE if sample else QK_SCALE * LOG2E
                qn = proj_heads(h_act, w_in_even, j, 0, fox_w, gain=fox_q_gain[j], scale=q_scale, name="fox_q")
                kn, leaf["fk" + tag] = proj_heads(h_act, w_in_even, j, fox_w, fox_w, gain=fox_k_gain[j],
                                                  leaf=(n_fox_layers, n_fox), leaf_prev=leaf["fk" + tag], name="fox_k")
                vv, leaf["fv" + tag] = proj_heads(h_act, w_in_even, j, 2 * fox_w, fox_w,
                                                  leaf=(n_fox_layers, n_fox), leaf_prev=leaf["fv" + tag], name="fox_v")
                lf = proj_logsigmoid(h_act, wf3, j, bias)
                cproj = proj_f32(h_act, wc3, j, name="conv_proj")
                if not sample:
                    kb = fox_key_bias(lf.reshape(bp, tp, LANES), n_fox)
                    attn = fox_flash(qn.reshape(bp, tp, fox_w), kn.reshape(bp, tp, fox_w),
                                     vv.reshape(bp, tp, fox_w), kb, n_fox, hp=FOX_HEADS_PER_STEP).reshape(mp, fox_w)
                    yc, cs = conv_gate(cproj.reshape(bp, tp, 3 * conv_c), conv_w[j],
                                       jnp.zeros((bp, CONV_TAPS - 1, conv_c), F32), tp)
                    yc = yc.reshape(mp, conv_c)
                    small["fl_p"].append(lf.reshape(bp, tp, LANES)[..., :n_fox])
                    small["cs_p"].append(cs)
                else:
                    lf_s = lf.reshape(bs, ts, LANES)[..., :n_fox]
                    lf_new = _pad_rows(lf_s.reshape(bs, ts * n_fox, 1), PAGE).reshape(bs, 1, PAGE)
                    new_rows = lambda lv: _pad_rows(lv[j].reshape(bs, ts * n_fox, HEAD_DIM), PAGE)
                    attn = decode_fox(qn.reshape(bs, ts * n_fox, HEAD_DIM), fox_k_pool, fox_v_pool, suf_pool, tot_pool,
                                      j, page_table, new_rows(leaf["fk_s"]), new_rows(leaf["fv_s"]), lf_new, n_fox, ts,
                                      pps=FOX_PAGES_PER_STEP)
                    attn = attn.reshape(ms, fox_w).astype(BF16)
                    cpad = _pad_rows(cproj.reshape(bs, ts, 3 * conv_c), ts_pad)
                    yc, cs = conv_gate(cpad, conv_w[j], state_conv[j], ts)
                    yc = yc[:, :ts].reshape(ms, conv_c)
                    small["fl_s"].append(lf_s)
                    small["cs_s"].append(cs)
                results.append(proj_residual([attn, yc], w_out_even, j, x_res, name="even_out"))
            xp, xs = results
        else:
            lam_init = 0.8 - 0.6 * math.exp(-0.3 * i)
            lam = diff_lambda(diff_lq1[j], diff_lk1[j], diff_lq2[j], diff_lk2[j], lam_init)
            results = []
            for h_act, x_res, sample in ((hp, xp, False), (hs, xs, True)):
                tag = "_s" if sample else "_p"
                q_scale = QK_SCALE if sample else QK_SCALE * LOG2E
                qn = proj_heads(h_act, w_in_odd, j, 0, qk_w, gain=diff_q_gain[j], scale=q_scale, name="diff_q")
                kn, leaf["dk" + tag] = proj_heads(h_act, w_in_odd, j, qk_w, qk_w, gain=diff_k_gain[j],
                                                  leaf=(n_diff_layers, qk_w // HEAD_DIM), leaf_prev=leaf["dk" + tag],
                                                  name="diff_k")
                vv, leaf["dv" + tag] = proj_heads(h_act, w_in_odd, j, 2 * qk_w, v_w,
                                                  leaf=(n_diff_layers, v_w // HEAD_DIM), leaf_prev=leaf["dv" + tag],
                                                  leaf_rows=diff_v_rows, tm=512, tn=v_w, name="diff_v")
                if not sample:
                    o = diff_flash(qn.reshape(bp, tp, qk_w), kn.reshape(bp, tp, qk_w), vv.reshape(bp, tp, v_w),
                                   slopes, lam, diff_subln_gain[j], 1.0 - lam_init, n_diff,
                                   hp=DIFF_HEADS_PER_STEP)
                    o = o.reshape(mp, v_w)
                else:
                    split = lambda a: jnp.moveaxis(a.reshape(bs, ts, n_diff, 2, HEAD_DIM), 3, 1)
                    q_rows = split(qn).reshape(bs, 2 * ts * n_diff, HEAD_DIM)
                    k_new = _pad_rows(split(leaf["dk_s"][j]).reshape(bs, 2, ts * n_diff, HEAD_DIM), PAGE)
                    v_new = jnp.swapaxes(leaf["dv_s"][j].reshape(bs, ts, 2, n_diff, HEAD_DIM), 2, 3)
                    v_new = _pad_rows(v_new.reshape(bs, ts * n_diff, dv), PAGE)
                    o = decode_diff(q_rows, diff_k_pool, diff_v_pool, j, page_table, k_new, v_new,
                                    slope_lanes, lam, diff_subln_gain[j], 1.0 - lam_init, n_diff, ts,
                                    pps=DIFF_PAGES_PER_STEP)
                    o = o.reshape(ms, v_w).astype(BF16)
                results.append(proj_residual([o], w_out_odd, j, x_res, name="odd_out"))
            xp, xs = results
        if i + 1 < depth:
            xp, hp = mlp_residual(xp, mlp_norm_gain[i], w_up_bf, w_down_bf, i, next_gain=attn_norm_gain[i + 1])
            xs, hs = mlp_residual(xs, mlp_norm_gain[i], w_up_bf, w_down_bf, i, next_gain=attn_norm_gain[i + 1])
        else:
            xp = mlp_residual(xp, mlp_norm_gain[i], w_up_bf, w_down_bf, i)
            xs = mlp_residual(xs, mlp_norm_gain[i], w_up_bf, w_down_bf, i)

    st = {k: jnp.stack(v) for k, v in small.items()}
    fox_leaf = lambda a, b, t: a.reshape(n_fox_layers, b, t, n_fox, HEAD_DIM)
    dk_leaf = lambda a, b, t: a.reshape(n_diff_layers, b, t, n_diff, 2, HEAD_DIM)
    dv_leaf = lambda a, b, t: jnp.swapaxes(a.reshape(n_diff_layers, b, t, 2, n_diff, HEAD_DIM), 3, 4).reshape(
        n_diff_layers, b, t, n_diff, dv)
    return (xp.reshape(bp, tp, d), xs.reshape(bs, ts, d),
            fox_leaf(leaf["fk_p"], bp, tp), fox_leaf(leaf["fv_p"], bp, tp), st["fl_p"], st["cs_p"],
            dk_leaf(leaf["dk_p"], bp, tp), dv_leaf(leaf["dv_p"], bp, tp),
            fox_leaf(leaf["fk_s"], bs, ts), fox_leaf(leaf["fv_s"], bs, ts), st["fl_s"], st["cs_s"],
            dk_leaf(leaf["dk_s"], bs, ts), dv_leaf(leaf["dv_s"], bs, ts))
```
